```python
import math
import jax, jax.numpy as jnp
from jax import lax
import numpy as np

D_MODEL = 2048
BATCH = 1
SEQ = 16384
DEPTH = 2

GRID_W = 64
CTX_LEN = 256
EPS = 1e-6
N_MOD = 6

GROUP_W = D_MODEL // 4
D_MIX = 4 * GROUP_W

GDN_HEADS = 4
GDN_DK = GROUP_W // GDN_HEADS
GDN_DV = GROUP_W // GDN_HEADS
GDN_CONV = 3
GDN_CHUNK = 64
MLA_HEADS = 4
MLA_NOPE = 128
MLA_ROPE = 64
MLA_V = GROUP_W // MLA_HEADS
MLA_Q_RANK = 448
MLA_KV_RANK = 128
ATTN_BLOCK = 128
ROPE_BASE = 10000.0
HY_CH = GROUP_W
HY_CONV = 3
HY_EMB = 33
HY_FFN = 64
HY_FAST = 0.3
HY_SLOW = 1.5
HY_TARGET = 1e-2
RET_HEADS = 4
RET_DK = GROUP_W // RET_HEADS
RET_DV = GROUP_W // RET_HEADS
RET_CHUNK = 128
RET_DECAY_F = (5.0, 7.0, 9.0, 11.0)
RET_DECAY_B = (6.0, 8.0, 10.0, 12.0)
D_FF = 5632
N_EXPERTS = 8
TOP_K = 2

IN_SIZES = (3 * GDN_HEADS * GDN_DK, GDN_HEADS * GDN_DV, 2 * GDN_HEADS, 2 * GDN_HEADS,
            MLA_Q_RANK, MLA_KV_RANK, MLA_ROPE,
            3 * HY_CH,
            3 * RET_HEADS * RET_DK, RET_HEADS * RET_DV)
D_IN = sum(IN_SIZES)

kernel_name = "hybrid_flow_backbone_block"

f32 = jnp.float32


def rms_norm(x, w=None):
    xf = x.astype(f32)
    y = (xf * lax.rsqrt(jnp.mean(xf * xf, axis=-1, keepdims=True) + EPS)).astype(x.dtype)
    return y if w is None else y * w


def l2_norm(x):
    xf = x.astype(f32)
    return (xf * lax.rsqrt(jnp.sum(xf * xf, axis=-1, keepdims=True) + EPS)).astype(x.dtype)


def split_in(p):
    offs = [int(o) for o in np.cumsum(IN_SIZES)[:-1]]
    return jnp.split(p, offs, axis=-1)


def depthwise_conv_centred(x, w):
    K = w.shape[0]
    return lax.conv_general_dilated(x, w[:, None, :].astype(x.dtype), window_strides=(1,),
                                    padding=[(K // 2, K // 2)],
                                    dimension_numbers=('NWC', 'WIO', 'NWC'),
                                    feature_group_count=x.shape[-1])


def rotate_half(x):
    x1, x2 = jnp.split(x, 2, axis=-1)
    return jnp.concatenate([-x2, x1], axis=-1)


def rope_angles(pos, dim):
    inv = ROPE_BASE ** (-jnp.arange(0, dim, 2, dtype=f32) / dim)
    ang = pos.astype(f32)[:, None] * inv[None, :]
    ang = jnp.concatenate([ang, ang], axis=-1)
    return jnp.cos(ang), jnp.sin(ang)


def apply_rope(x, pos):
    cos, sin = rope_angles(pos, x.shape[-1])
    shp = (pos.shape[0],) + (1,) * (x.ndim - 3) + (x.shape[-1],)
    cos, sin = cos.reshape(shp).astype(x.dtype), sin.reshape(shp).astype(x.dtype)
    return x * cos + rotate_half(x) * sin


def axial_rope(x, rows, cols):
    half = x.shape[-1] // 2
    return jnp.concatenate([apply_rope(x[..., :half], rows), apply_rope(x[..., half:], cols)], axis=-1)


def bidirectional(scan_f, scan_b, ctx_f, lat_f, ctx_b, lat_b, S0):
    flip = lambda seq: tuple(jnp.flip(a, axis=2) for a in seq)
    oc_f, sc_f = scan_f(*ctx_f, S0)
    ol_f, _ = scan_f(*lat_f, sc_f)
    oc_b, sc_b = scan_b(*flip(ctx_b), S0)
    ol_b, _ = scan_b(*flip(lat_b), sc_b)
    return oc_f + jnp.flip(oc_b, axis=2), ol_f + jnp.flip(ol_b, axis=2)


def unit_lower_inverse(A):
    C = A.shape[-1]
    Nm = -A
    T = jnp.eye(C, dtype=A.dtype) + Nm
    P = Nm
    for _ in range(int(math.log2(C)) - 1):
        P = P @ P
        T = T + T @ P
    return T


def gated_delta_scan(q, k, v, g, beta, S0):
    out_dtype = v.dtype
    q, k, v, g, beta = (t.astype(f32) for t in (q, k, v, g, beta))
    B, H, L, dk = q.shape
    dv = v.shape[-1]
    C = GDN_CHUNK
    N = L // C
    q = q * (dk ** -0.5)
    chunk = lambda t: jnp.moveaxis(t.reshape((B, H, N, C) + t.shape[3:]), 2, 0)
    q, k, v, g, beta = (chunk(t) for t in (q, k, v, g, beta))
    gc = jnp.cumsum(g, axis=-1)
    causal = jnp.tril(jnp.ones((C, C), bool))
    strict = jnp.tril(jnp.ones((C, C), bool), -1)
    decay = jnp.exp(jnp.where(causal, gc[..., :, None] - gc[..., None, :], -jnp.inf))
    kb = k * beta[..., None]
    A = jnp.where(strict, jnp.einsum('nbhcd,nbhsd->nbhcs', kb, k) * decay, 0.0)
    T = unit_lower_inverse(A)
    u = T @ (v * beta[..., None])
    w = T @ (kb * jnp.exp(gc)[..., None])

    def step(S, inp):
        qi, ki, ui, wi, gi, di = inp
        v_new = ui - wi @ S
        attn = (qi @ jnp.swapaxes(ki, -1, -2)) * di
        o = (qi * jnp.exp(gi)[..., None]) @ S + attn @ v_new
        g_last = gi[..., -1:]
        S = S * jnp.exp(g_last)[..., None] + jnp.swapaxes(ki * jnp.exp(g_last - gi)[..., None], -1, -2) @ v_new
        return S, o

    S, o = lax.scan(step, S0.astype(f32), (q, k, u, w, gc, decay))
    o = jnp.moveaxis(o, 0, 2).reshape(B, H, L, dv)
    return o.astype(out_dtype), S


def retention_scan(q, k, v, S0, log_gamma):
    out_dtype = v.dtype
    q, k, v = (t.astype(f32) for t in (q, k, v))
    B, H, L, dk = q.shape
    dv = v.shape[-1]
    C = RET_CHUNK
    N = L // C
    idx = jnp.arange(C, dtype=f32)
    rel = idx[:, None] - idx[None, :]
    D = jnp.where(rel >= 0, jnp.exp(jnp.maximum(rel, 0.0)[None] * log_gamma[:, None, None]), 0.0)
    q_dec = jnp.exp((idx + 1.0)[None, :] * log_gamma[:, None])[None, :, :, None]
    k_dec = jnp.exp((C - 1.0 - idx)[None, :] * log_gamma[:, None])[None, :, :, None]
    c_dec = jnp.exp(C * log_gamma)[None, :, None, None]
    chunk = lambda t: jnp.moveaxis(t.reshape(B, H, N, C, t.shape[-1]), 2, 0)

    def step(S, inp):
        qi, ki, vi = inp
        inner = (qi @ jnp.swapaxes(ki, -1, -2)) * D[None]
        o = (qi @ S) * q_dec + inner @ vi
        S = S * c_dec + jnp.swapaxes(ki * k_dec, -1, -2) @ vi
        return S, o

    S, o = lax.scan(step, S0.astype(f32), (chunk(q), chunk(k), chunk(v)))
    o = jnp.moveaxis(o, 0, 2).reshape(B, H, L, dv)
    return o.astype(out_dtype), S


def gdn_prepare(qkv, a, b, conv_w, A_log, dt_bias):
    B, L, _ = qkv.shape
    qkv = jax.nn.silu(depthwise_conv_centred(qkv, conv_w))
    q, k, v = jnp.split(qkv, 3, axis=-1)
    heads = lambda t: t.reshape(B, L, GDN_HEADS, -1).transpose(0, 2, 1, 3)
    q, k, v = l2_norm(heads(q)), l2_norm(heads(k)), heads(v)
    a = a.astype(f32).reshape(B, L, 2, GDN_HEADS)
    b = b.astype(f32).reshape(B, L, 2, GDN_HEADS)
    g = -jnp.exp(A_log.astype(f32)) * jax.nn.softplus(a + dt_bias.astype(f32))
    beta = jax.nn.sigmoid(b)
    g, beta = g.transpose(2, 0, 3, 1), beta.transpose(2, 0, 3, 1)
    return (q, k, v, g[0], beta[0]), (q, k, v, g[1], beta[1])


def gdn_output(o, z, norm_w):
    B, H, L, dv = o.shape
    o = rms_norm(o.transpose(0, 2, 1, 3), norm_w) * jax.nn.silu(z.reshape(B, L, H, dv))
    return o.reshape(B, L, H * dv)


def ret_prepare(qkv, pos):
    B, L, _ = qkv.shape
    q, k, v = (t.reshape(B, L, RET_HEADS, -1) for t in jnp.split(qkv, 3, axis=-1))
    if pos is not None:
        q, k = apply_rope(q, pos), apply_rope(k, pos)
    k = k * (RET_DK ** -0.5)
    return tuple(t.transpose(0, 2, 1, 3) for t in (q, k, v))


def ret_output(o, gate):
    B, H, L, dv = o.shape
    o = rms_norm(o.transpose(0, 2, 1, 3)).reshape(B, L, H * dv)
    return o * jax.nn.silu(gate)


def mla_project(cq, ckv, kr, q_norm_w, w_uq, kv_norm_w, w_ukv, rows, cols):
    B, L, _ = cq.shape
    q = (rms_norm(cq, q_norm_w) @ w_uq).reshape(B, L, MLA_HEADS, MLA_NOPE + MLA_ROPE)
    kv = (rms_norm(ckv, kv_norm_w) @ w_ukv).reshape(B, L, MLA_HEADS, MLA_NOPE + MLA_V)
    q_nope, q_rope = q[..., :MLA_NOPE], q[..., MLA_NOPE:]
    k_nope, v = kv[..., :MLA_NOPE], kv[..., MLA_NOPE:]
    if rows is not None:
        q_rope = axial_rope(q_rope, rows, cols)
        kr = axial_rope(kr, rows, cols)
    k_rope = jnp.broadcast_to(kr[:, :, None, :], (B, L, MLA_HEADS, MLA_ROPE))
    return jnp.concatenate([q_nope, q_rope], -1), jnp.concatenate([k_nope, k_rope], -1), v


def block_attention(q, k, v):
    B, Lq, H, Dk = q.shape
    nb = Lq // ATTN_BLOCK
    qb = jnp.moveaxis(q.reshape(B, nb, ATTN_BLOCK, H, Dk), 1, 0)
    scale = Dk ** -0.5

    def one(qi):
        s = jnp.einsum('bqhd,bkhd->bhqk', qi, k).astype(f32) * scale
        p = jax.nn.softmax(s, axis=-1).astype(v.dtype)
        return jnp.einsum('bhqk,bkhd->bqhd', p, v)

    o = lax.map(one, qb)
    return jnp.moveaxis(o, 0, 1).reshape(B, Lq, H * v.shape[-1])


def hyena_filters(L, w1, b1, w2, b2, w3, b3, w4, freq):
    w1, b1, w2, b2, w3, b3, w4, freq = (t.astype(f32) for t in (w1, b1, w2, b2, w3, b3, w4, freq))
    p = jnp.arange(L, dtype=f32)
    bands = (HY_EMB - 1) // 2
    fr = jnp.linspace(1e-4, bands - 1, bands, dtype=f32)
    ang = (2.0 * math.pi / L) * p[:, None] * fr[None, :]
    z = jnp.concatenate([(p / (L - 1))[:, None], jnp.cos(ang), -jnp.sin(ang)], axis=-1)
    hdn = jnp.sin(freq * (z @ w1 + b1))
    hdn = jnp.sin(freq * (hdn @ w2 + b2))
    hdn = jnp.sin(freq * (hdn @ w3 + b3))
    h = hdn @ w4
    deltas = jnp.abs(jnp.linspace(math.log(HY_FAST) / HY_TARGET, math.log(HY_SLOW) / HY_TARGET, 2 * HY_CH, dtype=f32))
    dist = jnp.abs(p - L // 2) / L
    h = h * jnp.exp(-dist[:, None] * deltas[None, :])
    return h / (jnp.sum(jnp.abs(h), axis=0, keepdims=True) + EPS)


def fft_conv_centred(u, h):
    L = u.shape[1]
    n = 2 * L
    U = jnp.fft.rfft(u.astype(f32), n=n, axis=1)
    Hf = jnp.fft.rfft(h, n=n, axis=0)
    y = jnp.fft.irfft(U * Hf[None], n=n, axis=1)
    return y[:, L // 2: L // 2 + L].astype(u.dtype)


def hyena_mixer(u, conv_w, w1, b1, w2, b2, w3, b3, w4, freq, bias):
    L = u.shape[1]
    u = depthwise_conv_centred(u, conv_w)
    v, x1, x2 = jnp.split(u, 3, axis=-1)
    h = hyena_filters(L, w1, b1, w2, b2, w3, b3, w4, freq)
    z1 = x1 * (fft_conv_centred(v, h[:, :HY_CH]) + bias[0] * v)
    return x2 * (fft_conv_centred(z1, h[:, HY_CH:]) + bias[1] * z1)


def token_mixers(h_c, h_l, w_in, gdn_conv_w, gdn_A_log, gdn_dt_bias, gdn_norm_w,
                 mla_q_norm_w, mla_w_uq, mla_kv_norm_w, mla_w_ukv,
                 hy_conv_w, hy_w1, hy_b1, hy_w2, hy_b2, hy_w3, hy_b3, hy_w4, hy_freq, hy_bias,
                 rows, cols, pos, want_ctx):
    B, Lc, _ = h_c.shape
    P = jnp.concatenate([h_c, h_l], axis=1) @ w_in
    pc, pl = split_in(P[:, :Lc]), split_in(P[:, Lc:])

    cf, cb = gdn_prepare(pc[0], pc[2], pc[3], gdn_conv_w, gdn_A_log, gdn_dt_bias)
    lf, lb = gdn_prepare(pl[0], pl[2], pl[3], gdn_conv_w, gdn_A_log, gdn_dt_bias)
    S0 = jnp.zeros((B, GDN_HEADS, GDN_DK, GDN_DV), f32)
    gdn_c, gdn_l = bidirectional(gated_delta_scan, gated_delta_scan, cf, lf, cb, lb, S0)

    qc, kc, vc = mla_project(pc[4], pc[5], pc[6], mla_q_norm_w, mla_w_uq, mla_kv_norm_w, mla_w_ukv, None, None)
    ql, kl, vl = mla_project(pl[4], pl[5], pl[6], mla_q_norm_w, mla_w_uq, mla_kv_norm_w, mla_w_ukv, rows, cols)
    mla_l = block_attention(ql, jnp.concatenate([kc, kl], axis=1), jnp.concatenate([vc, vl], axis=1))

    hy_l = hyena_mixer(pl[7], hy_conv_w, hy_w1, hy_b1, hy_w2, hy_b2, hy_w3, hy_b3, hy_w4, hy_freq, hy_bias)

    lg_f = jnp.log1p(-jnp.exp2(-jnp.asarray(RET_DECAY_F, f32)))
    lg_b = jnp.log1p(-jnp.exp2(-jnp.asarray(RET_DECAY_B, f32)))
    rc = ret_prepare(pc[8], None)
    rl = ret_prepare(pl[8], pos)
    R0 = jnp.zeros((B, RET_HEADS, RET_DK, RET_DV), f32)
    ret_c, ret_l = bidirectional(lambda q, k, v, s: retention_scan(q, k, v, s, lg_f),
                                 lambda q, k, v, s: retention_scan(q, k, v, s, lg_b),
                                 rc, rl, rc, rl, R0)

    o_l = jnp.concatenate([gdn_output(gdn_l, pl[1], gdn_norm_w), mla_l, hy_l, ret_output(ret_l, pl[9])], axis=-1)
    if not want_ctx:
        return None, o_l
    mla_c = block_attention(qc, kc, vc)
    hy_c = hyena_mixer(pc[7], hy_conv_w, hy_w1, hy_b1, hy_w2, hy_b2, hy_w3, hy_b3, hy_w4, hy_freq, hy_bias)
    o_c = jnp.concatenate([gdn_output(gdn_c, pc[1], gdn_norm_w), mla_c, hy_c, ret_output(ret_c, pc[9])], axis=-1)
    return o_c, o_l


def swiglu(h, w1, w3, w2):
    return (jax.nn.silu(h @ w1) * (h @ w3)) @ w2


def moe_swiglu(h, router, w1, w3, w2):
    logits = (h @ router).astype(f32)
    top_v, top_i = lax.top_k(logits, TOP_K)
    wts = jax.nn.softmax(top_v, axis=-1)
    gates = jnp.sum(jax.nn.one_hot(top_i, N_EXPERTS, dtype=f32) * wts[..., None], axis=-2)
    y = jnp.zeros_like(h)
    for e in range(N_EXPERTS):
        y = y + gates[..., e:e + 1].astype(h.dtype) * swiglu(h, w1[e], w3[e], w2[e])
    return y


def setup_inputs(seed: int = 0) -> dict:
    key = jax.random.key(seed)
    ks = list(jax.random.split(key, 64))
    nrm = lambda shape, scale: jax.random.normal(ks.pop(), shape, f32) * scale
    gain = lambda shape: 1.0 + 0.05 * jax.random.normal(ks.pop(), shape, f32)
    n_dense = (DEPTH + 1) // 2
    n_moe = DEPTH // 2
    dt = jnp.exp(jax.random.uniform(ks.pop(), (DEPTH, 2, GDN_HEADS), f32, math.log(1e-3), math.log(1e-1)))
    a_log = jnp.log(jax.random.uniform(ks.pop(), (DEPTH, 2, GDN_HEADS), f32, 1.0, 16.0))
    return {
        "x": nrm((BATCH, SEQ, D_MODEL), 1.0),
        "c": nrm((BATCH, D_MODEL), 1.0),
        "ctx": nrm((BATCH, CTX_LEN, D_MODEL), 1.0),
        "c_ctx": nrm((D_MODEL,), 1.0),
        "norm1_w": gain((DEPTH, D_MODEL)),
        "norm2_w": gain((DEPTH, D_MODEL)),
        "mod_w": nrm((DEPTH, D_MODEL, N_MOD * D_MODEL), 0.5 * D_MODEL ** -0.5),
        "mod_b": nrm((DEPTH, N_MOD * D_MODEL), 0.02),
        "w_in": nrm((DEPTH, D_MODEL, D_IN), D_MODEL ** -0.5),
        "w_out": nrm((DEPTH, D_MIX, D_MODEL), D_MIX ** -0.5),
        "gdn_conv_w": nrm((DEPTH, GDN_CONV, 3 * GDN_HEADS * GDN_DK), GDN_CONV ** -0.5),
        "gdn_A_log": a_log,
        "gdn_dt_bias": dt + jnp.log(-jnp.expm1(-dt)),
        "gdn_norm_w": gain((DEPTH, GDN_DV)),
        "mla_q_norm_w": gain((DEPTH, MLA_Q_RANK)),
        "mla_w_uq": nrm((DEPTH, MLA_Q_RANK, MLA_HEADS * (MLA_NOPE + MLA_ROPE)), MLA_Q_RANK ** -0.5),
        "mla_kv_norm_w": gain((DEPTH, MLA_KV_RANK)),
        "mla_w_ukv": nrm((DEPTH, MLA_KV_RANK, MLA_HEADS * (MLA_NOPE + MLA_V)), MLA_KV_RANK ** -0.5),
        "hy_conv_w": nrm((DEPTH, HY_CONV, 3 * HY_CH), HY_CONV ** -0.5),
        "hy_w1": nrm((DEPTH, HY_EMB, HY_FFN), HY_EMB ** -0.5),
        "hy_b1": nrm((DEPTH, HY_FFN), 0.1),
        "hy_w2": nrm((DEPTH, HY_FFN, HY_FFN), HY_FFN ** -0.5),
        "hy_b2": nrm((DEPTH, HY_FFN), 0.1),
        "hy_w3": nrm((DEPTH, HY_FFN, HY_FFN), HY_FFN ** -0.5),
        "hy_b3": nrm((DEPTH, HY_FFN), 0.1),
        "hy_w4": nrm((DEPTH, HY_FFN, 2 * HY_CH), HY_FFN ** -0.5),
        "hy_freq": gain((DEPTH, HY_FFN)),
        "hy_bias": nrm((DEPTH, 2, HY_CH), 1.0),
        "ffn_w1": nrm((n_dense, D_MODEL, D_FF), D_MODEL ** -0.5),
        "ffn_w3": nrm((n_dense, D_MODEL, D_FF), D_MODEL ** -0.5),
        "ffn_w2": nrm((n_dense, D_FF, D_MODEL), D_FF ** -0.5),
        "moe_router": nrm((n_moe, D_MODEL, N_EXPERTS), D_MODEL ** -0.5),
        "moe_w1": nrm((n_moe, N_EXPERTS, D_MODEL, D_FF), D_MODEL ** -0.5),
        "moe_w3": nrm((n_moe, N_EXPERTS, D_MODEL, D_FF), D_MODEL ** -0.5),
        "moe_w2": nrm((n_moe, N_EXPERTS, D_FF, D_MODEL), D_FF ** -0.5),
        "final_norm_w": gain((D_MODEL,)),
    }


def reference(x, c, ctx, c_ctx, norm1_w, norm2_w, mod_w, mod_b, w_in, w_out,
              gdn_conv_w, gdn_A_log, gdn_dt_bias, gdn_norm_w,
              mla_q_norm_w, mla_w_uq, mla_kv_norm_w, mla_w_ukv,
              hy_conv_w, hy_w1, hy_b1, hy_w2, hy_b2, hy_w3, hy_b3, hy_w4, hy_freq, hy_bias,
              ffn_w1, ffn_w3, ffn_w2, moe_router, moe_w1, moe_w3, moe_w2, final_norm_w):
    B, L, _ = x.shape
    Lc = ctx.shape[1]
    ROWS = L // GRID_W
    rows = jnp.repeat(jnp.arange(ROWS, dtype=jnp.int32), GRID_W)
    cols = jnp.tile(jnp.arange(GRID_W, dtype=jnp.int32), ROWS)
    pos = jnp.arange(L, dtype=jnp.int32)
    xc = ctx
    for l in range(DEPTH):
        want_ctx = l < DEPTH - 1
        mod_lat = (jax.nn.silu(c) @ mod_w[l] + mod_b[l])[:, None, :]
        mod_ctx = (jax.nn.silu(c_ctx) @ mod_w[l] + mod_b[l])[None, None, :]
        sh1, sc1, g1, sh2, sc2, g2 = jnp.split(mod_lat, N_MOD, axis=-1)
        csh1, csc1, cg1, csh2, csc2, cg2 = jnp.split(mod_ctx, N_MOD, axis=-1)

        h_l = rms_norm(x, norm1_w[l]) * (1 + sc1) + sh1
        h_c = rms_norm(xc, norm1_w[l]) * (1 + csc1) + csh1
        o_c, o_l = token_mixers(h_c, h_l, w_in[l], gdn_conv_w[l], gdn_A_log[l], gdn_dt_bias[l], gdn_norm_w[l],
                                mla_q_norm_w[l], mla_w_uq[l], mla_kv_norm_w[l], mla_w_ukv[l],
                                hy_conv_w[l], hy_w1[l], hy_b1[l], hy_w2[l], hy_b2[l], hy_w3[l], hy_b3[l],
                                hy_w4[l], hy_freq[l], hy_bias[l], rows, cols, pos, want_ctx)
        x = x + g1 * (o_l @ w_out[l])
        h_l = rms_norm(x, norm2_w[l]) * (1 + sc2) + sh2
        if want_ctx:
            xc = xc + cg1 * (o_c @ w_out[l])
            h_c = rms_norm(xc, norm2_w[l]) * (1 + csc2) + csh2
            h_all = jnp.concatenate([h_c, h_l], axis=1)
        else:
            h_all = h_l
        if l % 2 == 0:
            f = swiglu(h_all, ffn_w1[l // 2], ffn_w3[l // 2], ffn_w2[l // 2])
        else:
            f = moe_swiglu(h_all, moe_router[l // 2], moe_w1[l // 2], moe_w3[l // 2], moe_w2[l // 2])
        x = x + g2 * f[:, f.shape[1] - L:]
        if want_ctx:
            xc = xc + cg2 * f[:, :Lc]
    return rms_norm(x, final_norm_w)
```

```python
import functools
import math

import numpy as np
import jax
import jax.numpy as jnp
from jax import lax
from jax.experimental import pallas as pl
from jax.experimental.pallas import tpu as pltpu

F32 = jnp.float32
BF16 = jnp.bfloat16
HI = lax.Precision.HIGHEST

EPS = 1e-6
GRID_W = 64
N_MOD = 6
HEAD_W = 128
N_HEADS = 4
GROUP_W = N_HEADS * HEAD_W
GDN_CHUNK = 64
GDN_CONV = 3
MLA_NOPE = 128
MLA_ROPE = 64
MLA_Q_RANK = 448
MLA_KV_RANK = 128
ROPE_BASE = 10000.0
HY_EMB = 33
HY_FFN = 64
HY_FAST = 0.3
HY_SLOW = 1.5
HY_TARGET = 1e-2
RET_CHUNK = 128
RET_DECAY_F = (5.0, 7.0, 9.0, 11.0)
RET_DECAY_B = (6.0, 8.0, 10.0, 12.0)
N_EXPERTS = 8
FFT_N2 = 256
ROW_TILE = 256

C_GQKV = 0
C_HY = 1536
C_RQKV = 3072
C_GZ = 4608
C_RG = 5120
C_CQ = 5632
C_GAB = 6144
C_CKV = 6272
C_KR = 6400
P_COLS = 6656

VMEM_LIMIT = 56 * 1024 * 1024
NT_DIMS = (((1,), (1,)), ((), ()))
TN_DIMS = (((0,), (0,)), ((), ()))


def _cparams(sem):
    return pltpu.CompilerParams(dimension_semantics=sem, vmem_limit_bytes=VMEM_LIMIT)


def _tile(n, pref, unit=ROW_TILE):
    best = unit
    t = unit
    while t <= min(n, pref):
        if n % t == 0:
            best = t
        t += unit
    return best


def _silu(x):
    return x * jax.nn.sigmoid(x)


def _bdot(a, b):
    return jnp.dot(a.astype(BF16), b.astype(BF16), preferred_element_type=F32)


def _bdot_nt(a, b):
    return lax.dot_general(a.astype(BF16), b.astype(BF16), NT_DIMS, preferred_element_type=F32)


def _bdot_tn(a, b):
    return lax.dot_general(a.astype(BF16), b.astype(BF16), TN_DIMS, preferred_element_type=F32)


def _mm_kernel(*refs, n_extra, epi, a_pro, precision, tm, order):
    a_ref, b_ref = refs[0], refs[1]
    extras = refs[2:2 + n_extra]
    o_ref = refs[2 + n_extra]
    a = a_ref[...]
    if a_pro is not None:
        a = a_pro(a)
    if precision is None:
        acc = jnp.dot(a.astype(BF16), b_ref[...].astype(BF16), preferred_element_type=F32)
    else:
        acc = jnp.dot(a.astype(F32), b_ref[...].astype(F32), preferred_element_type=F32, precision=precision)
    if epi is not None:
        m = pl.program_id(0 if order == "mn" else 1)
        acc = epi(acc, *[e[...] for e in extras], row0=m * tm)
    o_ref[...] = acc.astype(o_ref.dtype)


def _mm(a, b, *, tm, tn, out_dtype, k=None, m=None, a_col_blk=0, order="mn", extras=(), epi=None,
        a_pro=None, precision=None, name="mm"):
    M = a.shape[0] if m is None else m
    K = a.shape[1] if k is None else k
    N = b.shape[1]
    assert M % tm == 0 and N % tn == 0, (M, tm, N, tn)
    gm, gn = M // tm, N // tn
    if order == "mn":
        grid = (gm, gn)
        mi = lambda i, j: i
        ni = lambda i, j: j
    else:
        grid = (gn, gm)
        mi = lambda i, j: j
        ni = lambda i, j: i
    in_specs = [pl.BlockSpec((tm, K), lambda i, j: (mi(i, j), a_col_blk)),
                pl.BlockSpec((K, tn), lambda i, j: (0, ni(i, j)))]
    ops = [a, b]
    for arr, kind in extras:
        if kind == "full":
            in_specs.append(pl.BlockSpec((tm, tn), lambda i, j: (mi(i, j), ni(i, j))))
        elif kind == "col":
            in_specs.append(pl.BlockSpec((arr.shape[0], tn), lambda i, j: (0, ni(i, j))))
        elif kind == "row":
            in_specs.append(pl.BlockSpec((tm, arr.shape[1]), lambda i, j: (mi(i, j), 0)))
        else:
            raise ValueError(kind)
        ops.append(arr)
    kern = functools.partial(_mm_kernel, n_extra=len(extras), epi=epi, a_pro=a_pro, precision=precision,
                             tm=tm, order=order)
    return pl.pallas_call(
        kern,
        out_shape=jax.ShapeDtypeStruct((M, N), out_dtype),
        grid=grid,
        in_specs=in_specs,
        out_specs=pl.BlockSpec((tm, tn), lambda i, j: (mi(i, j), ni(i, j))),
        compiler_params=_cparams(("parallel", "parallel")),
        name=name,
    )(*ops)


def _norm_mod_kernel(*refs, with_router, d):
    if with_router:
        x_ref, w_ref, sc_ref, sh_ref, r_ref, h_ref, g_ref = refs
    else:
        x_ref, w_ref, sc_ref, sh_ref, h_ref = refs
    x = x_ref[...]
    y = x * lax.rsqrt(jnp.sum(x * x, axis=-1, keepdims=True) * (1.0 / d) + EPS)
    h = (y * w_ref[...]) * (1.0 + sc_ref[...]) + sh_ref[...]
    h_ref[...] = h.astype(h_ref.dtype)
    if with_router:
        logits = jnp.dot(h, r_ref[...], preferred_element_type=F32, precision=HI)
        lane = lax.broadcasted_iota(jnp.int32, logits.shape, 1)
        neg = jnp.float32(-jnp.inf)
        lg = jnp.where(lane < N_EXPERTS, logits, neg)
        m1 = jnp.max(lg, axis=-1, keepdims=True)
        i1 = jnp.min(jnp.where(lg == m1, lane, 128), axis=-1, keepdims=True)
        lg2 = jnp.where(lane == i1, neg, lg)
        m2 = jnp.max(lg2, axis=-1, keepdims=True)
        i2 = jnp.min(jnp.where(lg2 == m2, lane, 128), axis=-1, keepdims=True)
        e2 = jnp.exp(m2 - m1)
        den = 1.0 + e2
        g_ref[...] = jnp.where(lane == i1, 1.0 / den, 0.0) + jnp.where(lane == i2, e2 / den, 0.0)


def _norm_mod(x, w, mod3, sc_blk, sh_blk, *, n_lat_tiles, out_dtype, router=None, rows=None):
    R = x.shape[0] if rows is None else rows
    D = x.shape[1]
    T = ROW_TILE
    which = lambda i: jnp.where(i >= n_lat_tiles, 1, 0)
    in_specs = [pl.BlockSpec((T, D), lambda i: (i, 0)),
                pl.BlockSpec((1, D), lambda i: (0, 0)),
                pl.BlockSpec((None, 1, D), lambda i: (which(i), 0, sc_blk)),
                pl.BlockSpec((None, 1, D), lambda i: (which(i), 0, sh_blk))]
    ops = [x, w, mod3, mod3]
    out_shape = [jax.ShapeDtypeStruct((R, D), out_dtype)]
    out_specs = [pl.BlockSpec((T, D), lambda i: (i, 0))]
    if router is not None:
        in_specs.append(pl.BlockSpec((D, 128), lambda i: (0, 0)))
        ops.append(router)
        out_shape.append(jax.ShapeDtypeStruct((R, 128), F32))
        out_specs.append(pl.BlockSpec((T, 128), lambda i: (i, 0)))
    res = pl.pallas_call(
        functools.partial(_norm_mod_kernel, with_router=router is not None, d=D),
        out_shape=out_shape, grid=(R // T,), in_specs=in_specs, out_specs=out_specs,
        compiler_params=_cparams(("parallel",)), name="norm_mod",
    )(*ops)
    return res if router is not None else res[0]


def _conv3_kernel(x_ref, prev_ref, next_ref, w_ref, o0, o1, o2, *, n_lat_tiles, gdn):
    i = pl.program_id(0)
    x = x_ref[...]
    T = x.shape[0]
    row = lax.broadcasted_iota(jnp.int32, x.shape, 0)
    has_prev = jnp.logical_and(i != 0, i != n_lat_tiles)
    has_next = jnp.logical_and(i != n_lat_tiles - 1, i != n_lat_tiles)
    prev_row = jnp.where(has_prev, prev_ref[7:8, :], 0.0)
    next_row = jnp.where(has_next, next_ref[0:1, :], 0.0)
    xm = jnp.where(row == 0, prev_row, pltpu.roll(x, 1, 0))
    xp = jnp.where(row == T - 1, next_row, pltpu.roll(x, T - 1, 0))
    w = w_ref[...]
    y = xm * w[0:1, :] + x * w[1:2, :] + xp * w[2:3, :]
    outs = (o0, o1, o2)
    if not gdn:
        for p in range(3):
            outs[p][...] = y[:, p * GROUP_W:(p + 1) * GROUP_W]
        return
    y = _silu(y)
    for p in range(3):
        for h in range(N_HEADS):
            c0 = p * GROUP_W + h * HEAD_W
            t = y[:, c0:c0 + HEAD_W]
            if p < 2:
                t = t * lax.rsqrt(jnp.sum(t * t, axis=-1, keepdims=True) + EPS)
                if p == 0:
                    t = t * (HEAD_W ** -0.5)
            outs[p][:, h * HEAD_W:(h + 1) * HEAD_W] = t


def _conv3(P, w, col0, *, n_lat_tiles, gdn):
    R = P.shape[0]
    T = ROW_TILE
    W = 3 * GROUP_W
    cb = col0 // W
    assert col0 % W == 0
    nb8 = R // 8
    tpb = T // 8
    in_specs = [pl.BlockSpec((T, W), lambda i: (i, cb)),
                pl.BlockSpec((8, W), lambda i: (jnp.maximum(i * tpb - 1, 0), cb)),
                pl.BlockSpec((8, W), lambda i: (jnp.minimum((i + 1) * tpb, nb8 - 1), cb)),
                pl.BlockSpec((GDN_CONV, W), lambda i: (0, 0))]
    return pl.pallas_call(
        functools.partial(_conv3_kernel, n_lat_tiles=n_lat_tiles, gdn=gdn),
        out_shape=[jax.ShapeDtypeStruct((R, GROUP_W), F32)] * 3,
        grid=(R // T,), in_specs=in_specs,
        out_specs=[pl.BlockSpec((T, GROUP_W), lambda i: (i, 0))] * 3,
        compiler_params=_cparams(("parallel",)), name="conv3_gdn" if gdn else "conv3_hy",
    )(P, P, P, w)


def _gdn_kernel(qf, kf, vf, gf, qb, kb_, vb, gb, alog_ref, dtb_ref, of_ref, ob_ref, s_scr):
    C = GDN_CHUNK

    @pl.when(pl.program_id(0) == 0)
    def _():
        s_scr[...] = jnp.zeros_like(s_scr)

    r = lax.broadcasted_iota(jnp.int32, (C, C), 0)
    c = lax.broadcasted_iota(jnp.int32, (C, C), 1)
    eye = (r == c).astype(F32)
    lane = lax.broadcasted_iota(jnp.int32, (C, 128), 1)
    alog = alog_ref[...]
    dtb = dtb_ref[...]
    dirs = ((qf, kf, vf, gf, of_ref, r >= c, r > c), (qb, kb_, vb, gb, ob_ref, r <= c, r < c))
    for d, (q_ref, k_ref, v_ref, g_ref, o_ref, incl, strict) in enumerate(dirs):
        x = g_ref[...]
        t = x + dtb
        softplus = jnp.maximum(t, 0.0) + jnp.log1p(jnp.exp(-jnp.abs(t)))
        g_all = -jnp.exp(alog) * softplus
        beta_all = jax.nn.sigmoid(x)
        cs_all = jnp.dot(incl.astype(F32), g_all, preferred_element_type=F32, precision=HI)
        tot_all = jnp.sum(g_all, axis=0, keepdims=True)
        for h in range(N_HEADS):
            l = N_HEADS * d + h
            gcol = cs_all[:, l:l + 1]
            bcol = beta_all[:, 8 + l:9 + l]
            tot = tot_all[:, l:l + 1]
            q = q_ref[:, h * HEAD_W:(h + 1) * HEAD_W]
            k = k_ref[:, h * HEAD_W:(h + 1) * HEAD_W]
            v = v_ref[:, h * HEAD_W:(h + 1) * HEAD_W]
            u_mat = jnp.where(lane == 0, gcol, jnp.where(lane == 1, 1.0, 0.0))
            v_mat = jnp.where(lane == 0, 1.0, jnp.where(lane == 1, -gcol, 0.0))
            diff = lax.dot_general(u_mat, v_mat, NT_DIMS, preferred_element_type=F32, precision=HI)
            decay = jnp.exp(jnp.where(incl, diff, -1e30))
            kbeta = k * bcol
            a_mat = jnp.where(strict, _bdot_nt(kbeta, k) * decay, 0.0)
            n_mat = -a_mat
            t_mat = eye + n_mat
            p_mat = n_mat
            for _ in range(int(math.log2(C)) - 1):
                p_mat = _bdot(p_mat, p_mat)
                t_mat = t_mat + _bdot(t_mat, p_mat)
            u = _bdot(t_mat, v * bcol)
            w = _bdot(t_mat, kbeta * jnp.exp(gcol))
            s = s_scr[l]
            v_new = u - _bdot(w, s)
            attn = _bdot_nt(q, k) * decay
            o = _bdot(q * jnp.exp(gcol), s) + _bdot(attn, v_new)
            s_scr[l] = s * jnp.exp(tot) + _bdot_tn(k * jnp.exp(tot - gcol), v_new)
            o_ref[:, h * HEAD_W:(h + 1) * HEAD_W] = o


def _gdn_scan(q, k, v, P, alog_row, dtb_row, *, n_lat):
    R = q.shape[0]
    C = GDN_CHUNK
    nt = R // C
    nl = n_lat // C
    gcb = C_GAB // 128
    fwd = lambda i: ((i + nl) % nt, 0)
    bwd = lambda i: (nt - 1 - i, 0)
    gfwd = lambda i: ((i + nl) % nt, gcb)
    gbwd = lambda i: (nt - 1 - i, gcb)
    blk = lambda im: pl.BlockSpec((C, GROUP_W), im)
    in_specs = [blk(fwd), blk(fwd), blk(fwd), pl.BlockSpec((C, 128), gfwd),
                blk(bwd), blk(bwd), blk(bwd), pl.BlockSpec((C, 128), gbwd),
                pl.BlockSpec((1, 128), lambda i: (0, 0)), pl.BlockSpec((1, 128), lambda i: (0, 0))]
    return pl.pallas_call(
        _gdn_kernel,
        out_shape=[jax.ShapeDtypeStruct((R, GROUP_W), F32)] * 2,
        grid=(nt,), in_specs=in_specs, out_specs=[blk(fwd), blk(bwd)],
        scratch_shapes=[pltpu.VMEM((2 * N_HEADS, HEAD_W, HEAD_W), F32)],
        compiler_params=_cparams(("arbitrary",)), name="gdn_scan",
    )(q, k, v, P, q, k, v, P, alog_row, dtb_row)


def _ret_kernel(qf, kf, vf, cf, sf, qb, kb_, vb, cb, sb, dmat_ref, qdec_ref, kdec_ref, cdec_ref,
                of_ref, ob_ref, s_scr):
    @pl.when(pl.program_id(0) == 0)
    def _():
        s_scr[...] = jnp.zeros_like(s_scr)

    dirs = ((qf, kf, vf, cf, sf, of_ref), (qb, kb_, vb, cb, sb, ob_ref))
    for d, (q_ref, k_ref, v_ref, cos_ref, sin_ref, o_ref) in enumerate(dirs):
        cos = cos_ref[...]
        sin = sin_ref[...]
        for h in range(N_HEADS):
            l = N_HEADS * d + h
            sl = slice(h * HEAD_W, (h + 1) * HEAD_W)
            q = q_ref[:, sl]
            k = k_ref[:, sl]
            v = v_ref[:, sl]
            q = q * cos + pltpu.roll(q, HEAD_W // 2, 1) * sin
            k = (k * cos + pltpu.roll(k, HEAD_W // 2, 1) * sin) * (HEAD_W ** -0.5)
            s = s_scr[l]
            inner = _bdot_nt(q, k) * dmat_ref[l]
            o = _bdot(q, s) * qdec_ref[l] + _bdot(inner, v)
            s_scr[l] = s * cdec_ref[l] + _bdot_tn(k * kdec_ref[l], v)
            o_ref[:, sl] = o


def _ret_tables():
    C = RET_CHUNK
    lg = np.concatenate([np.log1p(-np.exp2(-np.asarray(RET_DECAY_F, np.float32)).astype(np.float32)),
                         np.log1p(-np.exp2(-np.asarray(RET_DECAY_B, np.float32)).astype(np.float32))]).astype(np.float64)
    idx = np.arange(C, dtype=np.float64)
    rel = idx[:, None] - idx[None, :]
    dmat = np.zeros((8, C, C))
    qdec = np.zeros((8, C, HEAD_W))
    kdec = np.zeros((8, C, HEAD_W))
    cdec = np.zeros((8, C, HEAD_W))
    for l in range(8):
        low = np.where(rel >= 0, np.exp(np.maximum(rel, 0.0) * lg[l]), 0.0)
        if l < N_HEADS:
            dmat[l] = low
            qdec[l] = np.exp((idx + 1.0) * lg[l])[:, None]
            kdec[l] = np.exp((C - 1.0 - idx) * lg[l])[:, None]
        else:
            dmat[l] = low.T
            qdec[l] = np.exp((C - idx) * lg[l])[:, None]
            kdec[l] = np.exp(idx * lg[l])[:, None]
        cdec[l] = np.exp(C * lg[l])
    return tuple(jnp.asarray(t, F32) for t in (dmat, qdec, kdec, cdec))


def _ret_scan(P, cos_t, sin_t, *, n_lat):
    R = P.shape[0]
    C = RET_CHUNK
    nt = R // C
    nl = n_lat // C
    cb0 = C_RQKV // GROUP_W
    fi = lambda i: (i + nl) % nt
    bi = lambda i: nt - 1 - i
    pblk = lambda f, c: pl.BlockSpec((C, GROUP_W), lambda i: (f(i), c))
    tblk = lambda f: pl.BlockSpec((C, HEAD_W), lambda i: (f(i), 0))
    cst = lambda: pl.BlockSpec((8, C, HEAD_W), lambda i: (0, 0, 0))
    in_specs = [pblk(fi, cb0), pblk(fi, cb0 + 1), pblk(fi, cb0 + 2), tblk(fi), tblk(fi),
                pblk(bi, cb0), pblk(bi, cb0 + 1), pblk(bi, cb0 + 2), tblk(bi), tblk(bi),
                cst(), cst(), cst(), cst()]
    dmat, qdec, kdec, cdec = _ret_tables()
    oblk = lambda f: pl.BlockSpec((C, GROUP_W), lambda i: (f(i), 0))
    return pl.pallas_call(
        _ret_kernel,
        out_shape=[jax.ShapeDtypeStruct((R, GROUP_W), F32)] * 2,
        grid=(nt,), in_specs=in_specs, out_specs=[oblk(fi), oblk(bi)],
        scratch_shapes=[pltpu.VMEM((2 * N_HEADS, HEAD_W, HEAD_W), F32)],
        compiler_params=_cparams(("arbitrary",)), name="ret_scan",
    )(P, P, P, cos_t, sin_t, P, P, P, cos_t, sin_t, dmat, qdec, kdec, cdec)


def _headnorm_kernel(a_ref, b_ref, g_ref, w_ref, o_ref):
    w = w_ref[...]
    for h in range(N_HEADS):
        sl = slice(h * HEAD_W, (h + 1) * HEAD_W)
        o = a_ref[:, sl] + b_ref[:, sl]
        y = o * lax.rsqrt(jnp.sum(o * o, axis=-1, keepdims=True) * (1.0 / HEAD_W) + EPS)
        o_ref[:, sl] = ((y * w) * _silu(g_ref[:, sl])).astype(o_ref.dtype)


def _headnorm_gate(a, b, P, gate_col0, w_row):
    R = a.shape[0]
    T = ROW_TILE
    gb = gate_col0 // GROUP_W
    return pl.pallas_call(
        _headnorm_kernel,
        out_shape=jax.ShapeDtypeStruct((R, GROUP_W), BF16),
        grid=(R // T,),
        in_specs=[pl.BlockSpec((T, GROUP_W), lambda i: (i, 0)), pl.BlockSpec((T, GROUP_W), lambda i: (i, 0)),
                  pl.BlockSpec((T, GROUP_W), lambda i: (i, gb)), pl.BlockSpec((1, HEAD_W), lambda i: (0, 0))],
        out_specs=pl.BlockSpec((T, GROUP_W), lambda i: (i, 0)),
        compiler_params=_cparams(("parallel",)), name="headnorm_gate",
    )(a, b, P, w_row)


def _mla_proj_kernel(cq_ref, ckv_ref, kr_ref, cos_ref, sin_ref, qw_ref, kvw_ref, wuq_ref, wukv_ref,
                     q_out, k_out, v_out):
    cq = cq_ref[...]
    nq = cq * lax.rsqrt(jnp.sum(cq * cq, axis=-1, keepdims=True) * (1.0 / MLA_Q_RANK) + EPS) * qw_ref[...]
    ckv = ckv_ref[...]
    nkv = ckv * lax.rsqrt(jnp.sum(ckv * ckv, axis=-1, keepdims=True) * (1.0 / MLA_KV_RANK) + EPS) * kvw_ref[...]
    q = _bdot(nq, wuq_ref[...])
    kv = _bdot(nkv, wukv_ref[...])
    cos = cos_ref[...]
    sin = sin_ref[...]
    kr = kr_ref[...]
    kr = kr * cos + pltpu.roll(kr, 64, 1) * sin
    scale = (MLA_NOPE + MLA_ROPE) ** -0.5
    for h in range(N_HEADS):
        q_out[:, 256 * h:256 * h + 128] = (q[:, 256 * h:256 * h + 128] * scale).astype(q_out.dtype)
        qr = q[:, 256 * h + 128:256 * h + 256]
        qr = qr * cos + pltpu.roll(qr, 64, 1) * sin
        q_out[:, 256 * h + 128:256 * h + 256] = (qr * scale).astype(q_out.dtype)
        k_out[:, 256 * h:256 * h + 128] = kv[:, 256 * h:256 * h + 128].astype(k_out.dtype)
        k_out[:, 256 * h + 128:256 * h + 256] = kr.astype(k_out.dtype)
        v_out[:, 128 * h:128 * h + 128] = kv[:, 256 * h + 128:256 * h + 256].astype(v_out.dtype)


def _mla_proj(P, cos_t, sin_t, qw_row, kvw_row, wuq_ext, wukv):
    R = P.shape[0]
    T = ROW_TILE
    cst = lambda s: pl.BlockSpec(s, lambda i: (0, 0))
    return pl.pallas_call(
        _mla_proj_kernel,
        out_shape=[jax.ShapeDtypeStruct((R, 1024), BF16), jax.ShapeDtypeStruct((R, 1024), BF16),
                   jax.ShapeDtypeStruct((R, GROUP_W), BF16)],
        grid=(R // T,),
        in_specs=[pl.BlockSpec((T, 512), lambda i: (i, C_CQ // 512)),
                  pl.BlockSpec((T, 128), lambda i: (i, C_CKV // 128)),
                  pl.BlockSpec((T, 128), lambda i: (i, C_KR // 128)),
                  pl.BlockSpec((T, 128), lambda i: (i, 0)), pl.BlockSpec((T, 128), lambda i: (i, 0)),
                  cst((1, 512)), cst((1, 128)), cst((512, 1024)), cst((128, 1024))],
        out_specs=[pl.BlockSpec((T, 1024), lambda i: (i, 0)), pl.BlockSpec((T, 1024), lambda i: (i, 0)),
                   pl.BlockSpec((T, GROUP_W), lambda i: (i, 0))],
        compiler_params=_cparams(("parallel",)), name="mla_proj",
    )(P, P, P, cos_t, sin_t, qw_row, kvw_row, wuq_ext, wukv)


def _flash_kernel(q_ref, k_ref, v_ref, o_ref, m_scr, l_scr, acc_scr, *, nk, tk, n_lat, n_lat_qtiles):
    i = pl.program_id(1)
    j = pl.program_id(2)

    @pl.when(j == 0)
    def _():
        m_scr[...] = jnp.full_like(m_scr, -jnp.inf)
        l_scr[...] = jnp.zeros_like(l_scr)
        acc_scr[...] = jnp.zeros_like(acc_scr)

    def body(ctx_only):
        s = lax.dot_general(q_ref[...], k_ref[...], NT_DIMS, preferred_element_type=F32)
        if ctx_only:
            kidx = j * tk + lax.broadcasted_iota(jnp.int32, s.shape, 1)
            s = jnp.where(kidx >= n_lat, s, -1e30)
        m_prev = m_scr[...]
        m_new = jnp.maximum(m_prev, jnp.max(s, axis=-1, keepdims=True))
        alpha = jnp.exp(m_prev - m_new)
        p = jnp.exp(s - m_new)
        l_scr[...] = alpha * l_scr[...] + jnp.sum(p, axis=-1, keepdims=True)
        acc_scr[...] = alpha * acc_scr[...] + jnp.dot(p.astype(BF16), v_ref[...], preferred_element_type=F32)
        m_scr[...] = m_new

    @pl.when(i < n_lat_qtiles)
    def _():
        body(False)

    @pl.when(jnp.logical_and(i >= n_lat_qtiles, (j + 1) * tk > n_lat))
    def _():
        body(True)

    @pl.when(j == nk - 1)
    def _():
        o_ref[...] = (acc_scr[...] / l_scr[...]).astype(o_ref.dtype)


def _flash(qh, kh, vh, *, n_lat, tq, tk):
    R = qh.shape[0]
    assert R % tq == 0 and R % tk == 0 and n_lat % tq == 0
    nq, nk = R // tq, R // tk
    return pl.pallas_call(
        functools.partial(_flash_kernel, nk=nk, tk=tk, n_lat=n_lat, n_lat_qtiles=n_lat // tq),
        out_shape=jax.ShapeDtypeStruct((R, GROUP_W), BF16),
        grid=(N_HEADS, nq, nk),
        in_specs=[pl.BlockSpec((tq, 256), lambda h, i, j: (i, h)),
                  pl.BlockSpec((tk, 256), lambda h, i, j: (j, h)),
                  pl.BlockSpec((tk, 128), lambda h, i, j: (j, h))],
        out_specs=pl.BlockSpec((tq, 128), lambda h, i, j: (i, h)),
        scratch_shapes=[pltpu.VMEM((tq, 1), F32), pltpu.VMEM((tq, 1), F32), pltpu.VMEM((tq, 128), F32)],
        compiler_params=_cparams(("parallel", "parallel", "arbitrary")), name="mla_attention",
    )(qh, kh, vh)


def _hy_mlp_kernel(z_ref, w1, b1, w2, b2, w3, b3, fr, o_ref):
    f = fr[...]
    hdot = lambda a, b: jnp.dot(a, b, preferred_element_type=F32, precision=HI)
    h = jnp.sin(f * (hdot(z_ref[...], w1[...]) + b1[...]))
    h = jnp.sin(f * (hdot(h, w2[...]) + b2[...]))
    o_ref[...] = jnp.sin(f * (hdot(h, w3[...]) + b3[...]))


def _hy_mlp(z, w1p, b1, w2, b2, w3, b3, fr):
    L = z.shape[0]
    T = _tile(L, 2048)
    cst = lambda s: pl.BlockSpec(s, lambda i: (0, 0))
    return pl.pallas_call(
        _hy_mlp_kernel, out_shape=jax.ShapeDtypeStruct((L, HY_FFN), F32), grid=(L // T,),
        in_specs=[pl.BlockSpec((T, 128), lambda i: (i, 0)), cst((128, HY_FFN)), cst((1, HY_FFN)),
                  cst((HY_FFN, HY_FFN)), cst((1, HY_FFN)), cst((HY_FFN, HY_FFN)), cst((1, HY_FFN)),
                  cst((1, HY_FFN))],
        out_specs=pl.BlockSpec((T, HY_FFN), lambda i: (i, 0)),
        compiler_params=_cparams(("parallel",)), name="hy_filter_mlp",
    )(z, w1p, b1, w2, b2, w3, b3, fr)


def _hy_filter_kernel(hd_ref, w4_ref, dl_ref, o_ref, sum_scr, *, L, T):
    ph = pl.program_id(1)
    r = pl.program_id(2)
    h = jnp.dot(hd_ref[...], w4_ref[...], preferred_element_type=F32, precision=HI)
    p = (r * T + lax.broadcasted_iota(jnp.int32, (T, 1), 0)).astype(F32)
    dist = jnp.abs(p - (L // 2)) / L
    h = h * jnp.exp(-dist * dl_ref[...])

    @pl.when(jnp.logical_and(ph == 0, r == 0))
    def _():
        sum_scr[...] = jnp.zeros_like(sum_scr)

    @pl.when(ph == 0)
    def _():
        sum_scr[...] += jnp.sum(jnp.abs(h), axis=0, keepdims=True)

    @pl.when(ph == 1)
    def _():
        o_ref[...] = h / (sum_scr[...] + EPS)


def _hy_filter(hd, w4, deltas_row):
    L = hd.shape[0]
    C2 = w4.shape[1]
    T = _tile(L, 2048)
    tc = 256
    return pl.pallas_call(
        functools.partial(_hy_filter_kernel, L=L, T=T),
        out_shape=jax.ShapeDtypeStruct((L, C2), F32),
        grid=(C2 // tc, 2, L // T),
        in_specs=[pl.BlockSpec((T, HY_FFN), lambda c, p, r: (r, 0)),
                  pl.BlockSpec((HY_FFN, tc), lambda c, p, r: (0, c)),
                  pl.BlockSpec((1, tc), lambda c, p, r: (0, c))],
        out_specs=pl.BlockSpec((T, tc), lambda c, p, r: (r * p, c)),
        scratch_shapes=[pltpu.VMEM((1, tc), F32)],
        compiler_params=_cparams(("arbitrary", "arbitrary", "arbitrary")), name="hy_filter",
    )(hd, w4, deltas_row)


def _cplx_dot(g_ref, pr, pi, o_ref):
    gr = g_ref[0]
    gi = g_ref[1]
    dot = lambda a, b: jnp.dot(a, b, preferred_element_type=F32)
    o_ref[0] = (dot(gr, pr) - dot(gi, pi)).astype(o_ref.dtype)
    o_ref[1] = (dot(gi, pr) + dot(gr, pi)).astype(o_ref.dtype)


def _fft_inner_kernel(g_ref, a_ref, b_ref):
    _cplx_dot(g_ref, a_ref[0], a_ref[1], b_ref)


def _fft_inner_mul_kernel(g_ref, u_ref, h_ref, c_ref):
    ur = u_ref[0].astype(F32)
    ui = u_ref[1].astype(F32)
    hr = h_ref[0].astype(F32)
    hi = h_ref[1].astype(F32)
    pr = (ur * hr - ui * hi).astype(BF16)
    pi = (ur * hi + ui * hr).astype(BF16)
    _cplx_dot(g_ref, pr, pi, c_ref)


def _fft_inner(g, a4, *, tc=GROUP_W):
    _, n1, n2, C = a4.shape
    blk = pl.BlockSpec((2, None, n2, tc), lambda k, c: (0, k, 0, c))
    return pl.pallas_call(
        _fft_inner_kernel,
        out_shape=jax.ShapeDtypeStruct((2, n1, n2, C), BF16),
        grid=(n1, C // tc),
        in_specs=[pl.BlockSpec((None, 2, n2, n2), lambda k, c: (k, 0, 0, 0)), blk],
        out_specs=blk,
        compiler_params=_cparams(("parallel", "parallel")), name="fft_inner",
    )(g, a4)


def _fft_inner_mul(ginv, u4, h4, h_cblk, *, tc=GROUP_W):
    _, n1, n2, C = u4.shape
    assert C == tc
    ublk = pl.BlockSpec((2, None, n2, tc), lambda k: (0, k, 0, 0))
    hblk = pl.BlockSpec((2, None, n2, tc), lambda k: (0, k, 0, h_cblk))
    return pl.pallas_call(
        _fft_inner_mul_kernel,
        out_shape=jax.ShapeDtypeStruct((2, n1, n2, C), BF16),
        grid=(n1,),
        in_specs=[pl.BlockSpec((None, 2, n2, n2), lambda k: (k, 0, 0, 0)), ublk, hblk],
        out_specs=ublk,
        compiler_params=_cparams(("parallel",)), name="fft_inner_mul",
    )(ginv, u4, h4)


def _fft_tables(n1, k_rows, out_rows):
    n2 = FFT_N2
    N = n1 * n2
    k1 = np.arange(n1)
    ang1 = 2.0 * np.pi * ((k1[:, None] * np.arange(k_rows)[None, :]) % n1) / n1
    f1 = np.concatenate([np.cos(ang1), -np.sin(ang1)], axis=0)
    orow = np.asarray(out_rows)
    ang5 = 2.0 * np.pi * ((orow[:, None] * k1[None, :]) % n1) / n1
    f5 = np.concatenate([np.cos(ang5), -np.sin(ang5)], axis=1) / N
    i2 = jnp.arange(n2, dtype=jnp.int32)
    a1 = (2.0 * math.pi / N) * (jnp.arange(n1, dtype=jnp.int32)[:, None] * i2[None, :]).astype(F32)
    a2 = (2.0 * math.pi / n2) * ((i2[:, None] * i2[None, :]) % n2).astype(F32)
    c1, s1, c2, s2 = jnp.cos(a1), jnp.sin(a1), jnp.cos(a2), jnp.sin(a2)
    gr = c1[:, None, :] * c2[None, :, :] - s1[:, None, :] * s2[None, :, :]
    gi = -(s1[:, None, :] * c2[None, :, :] + c1[:, None, :] * s2[None, :, :])
    g = jnp.stack([gr, gi], axis=1).astype(BF16)
    vr = c1[:, :, None] * c2[None, :, :] - s1[:, :, None] * s2[None, :, :]
    vi = s1[:, :, None] * c2[None, :, :] + c1[:, :, None] * s2[None, :, :]
    ginv = jnp.stack([vr, vi], axis=1).astype(BF16)
    return jnp.asarray(f1, BF16), g, ginv, jnp.asarray(f5, BF16)


def _fft_forward(x2d, f1, g, *, k_rows, n1, C, tc=GROUP_W):
    ncol = FFT_N2 * C
    a = _mm(f1, x2d, tm=2 * n1, tn=_tile(ncol, 8192, 128), out_dtype=BF16, k=k_rows, name="fft_outer_fwd")
    return _fft_inner(g, a.reshape(2, n1, FFT_N2, C), tc=tc)


def _gate_epi(acc, x, v, bias, *, row0):
    return x * (acc + bias * v)


def _resid_epi(acc, x, g_lat, g_ctx, *, row0, n_lat):
    row = row0 + lax.broadcasted_iota(jnp.int32, (acc.shape[0], 1), 0)
    return x + jnp.where(row < n_lat, g_lat, g_ctx) * acc


def _resid_gate_epi(acc, x, g_lat, g_ctx, gates, *, row0, n_lat, e):
    row = row0 + lax.broadcasted_iota(jnp.int32, (acc.shape[0], 1), 0)
    return x + (jnp.where(row < n_lat, g_lat, g_ctx) * gates[:, e:e + 1]) * acc


def _mix_out_kernel(a0, a1, a2, a3, w_ref, x_ref, gl_ref, gc_ref, o_ref, *, tm, n_lat):
    acc = jnp.dot(a0[...], w_ref[0:GROUP_W, :], preferred_element_type=F32)
    for g, a in ((1, a1), (2, a2), (3, a3)):
        acc = acc + jnp.dot(a[...], w_ref[g * GROUP_W:(g + 1) * GROUP_W, :], preferred_element_type=F32)
    o_ref[...] = _resid_epi(acc, x_ref[...], gl_ref[...], gc_ref[...], row0=pl.program_id(0) * tm, n_lat=n_lat)


def _mix_out(parts, w_out_b, x, mod3, g_blk, *, n_lat):
    R, D = x.shape
    tm = _tile(R, 1280)
    tn = _tile(D, 1024, 128)
    ablk = pl.BlockSpec((tm, GROUP_W), lambda i, j: (i, 0))
    gspec = lambda which: pl.BlockSpec((None, 1, tn), lambda i, j: (which, 0, g_blk * (D // tn) + j))
    return pl.pallas_call(
        functools.partial(_mix_out_kernel, tm=tm, n_lat=n_lat),
        out_shape=jax.ShapeDtypeStruct((R, D), F32),
        grid=(R // tm, D // tn),
        in_specs=[ablk, ablk, ablk, ablk, pl.BlockSpec((4 * GROUP_W, tn), lambda i, j: (0, j)),
                  pl.BlockSpec((tm, tn), lambda i, j: (i, j)), gspec(0), gspec(1)],
        out_specs=pl.BlockSpec((tm, tn), lambda i, j: (i, j)),
        compiler_params=_cparams(("parallel", "parallel")), name="mix_out",
    )(*parts, w_out_b, x, mod3, mod3)


def _swiglu_up_kernel(a_ref, w1_ref, w3_ref, o_ref):
    a = a_ref[...]
    u = jnp.dot(a, w1_ref[...], preferred_element_type=F32)
    g = jnp.dot(a, w3_ref[...], preferred_element_type=F32)
    o_ref[...] = (_silu(u) * g).astype(o_ref.dtype)


def _swiglu_up(h, w1, w3):
    R, D = h.shape
    F = w1.shape[1]
    tm = _tile(R, 1280)
    tn = _tile(F, 512, 128)
    return pl.pallas_call(
        _swiglu_up_kernel, out_shape=jax.ShapeDtypeStruct((R, F), BF16), grid=(R // tm, F // tn),
        in_specs=[pl.BlockSpec((tm, D), lambda i, j: (i, 0)), pl.BlockSpec((D, tn), lambda i, j: (0, j)),
                  pl.BlockSpec((D, tn), lambda i, j: (0, j))],
        out_specs=pl.BlockSpec((tm, tn), lambda i, j: (i, j)),
        compiler_params=_cparams(("parallel", "parallel")), name="swiglu_up",
    )(h, w1, w3)


def _pack_w_in(w_in, l):
    D = w_in.shape[0]
    offs = np.cumsum([0, 1536, 512, 8, 8, MLA_Q_RANK, MLA_KV_RANK, MLA_ROPE, 1536, 1536, 512])
    seg = lambda i: w_in[:, int(offs[i]):int(offs[i + 1])]
    z = lambda n: jnp.zeros((D, n), w_in.dtype)
    kr = seg(6)
    cols = [seg(0), seg(7), seg(8), seg(1), seg(9),
            seg(4), z(512 - MLA_Q_RANK),
            seg(2), seg(3), z(128 - 16),
            seg(5),
            kr, _rot_cols(kr),
            z(P_COLS - 6528)]
    return jnp.concatenate(cols, axis=1).astype(BF16)


def _rot_cols(w):
    out = []
    for b in range(2):
        blk = w[:, 32 * b:32 * b + 32]
        out += [-blk[:, 16:32], blk[:, 0:16]]
    return jnp.concatenate(out, axis=1)


def _pack_w_uq(w_uq):
    cols = []
    for h in range(N_HEADS):
        base = h * (MLA_NOPE + MLA_ROPE)
        rope = w_uq[:, base + MLA_NOPE:base + MLA_NOPE + MLA_ROPE]
        cols += [w_uq[:, base:base + MLA_NOPE], rope, _rot_cols(rope)]
    w = jnp.concatenate(cols, axis=1)
    return jnp.pad(w, ((0, 512 - MLA_Q_RANK), (0, 0))).astype(BF16)


def _rope_tables(L, Lc):
    def cs(pos, dim):
        inv = ROPE_BASE ** (-jnp.arange(0, dim, 2, dtype=F32) / dim)
        ang = pos.astype(F32)[:, None] * inv[None, :]
        ang = jnp.concatenate([ang, ang], axis=-1)
        return jnp.cos(ang), jnp.sin(ang)
    t = jnp.arange(L, dtype=jnp.int32)
    rows, cols = t // GRID_W, t % GRID_W
    cr, sr = cs(rows, MLA_ROPE // 2)
    cc, sc = cs(cols, MLA_ROPE // 2)
    z64 = jnp.zeros((L, 64), F32)
    mla_cos = jnp.concatenate([cr, cc, z64], axis=1)
    mla_sin = jnp.concatenate([sr, sc, z64], axis=1)
    ctx_cos = jnp.concatenate([jnp.ones((Lc, 64), F32), jnp.zeros((Lc, 64), F32)], axis=1)
    mla_cos = jnp.concatenate([mla_cos, ctx_cos], axis=0)
    mla_sin = jnp.concatenate([mla_sin, jnp.zeros((Lc, 128), F32)], axis=0)
    c, s = cs(t, HEAD_W)
    sign = jnp.concatenate([-jnp.ones((1, 64), F32), jnp.ones((1, 64), F32)], axis=1)
    ret_cos = jnp.concatenate([c, jnp.ones((Lc, HEAD_W), F32)], axis=0)
    ret_sin = jnp.concatenate([s * sign, jnp.zeros((Lc, HEAD_W), F32)], axis=0)
    return mla_cos, mla_sin, ret_cos, ret_sin


def _hy_features(L):
    p = jnp.arange(L, dtype=F32)
    bands = (HY_EMB - 1) // 2
    fr = jnp.linspace(1e-4, bands - 1, bands, dtype=F32)
    ang = (2.0 * math.pi / L) * p[:, None] * fr[None, :]
    z = jnp.concatenate([(p / (L - 1))[:, None], jnp.cos(ang), -jnp.sin(ang)], axis=-1)
    return jnp.pad(z, ((0, 0), (0, 128 - HY_EMB)))


def _hy_filters(L, w1, b1, w2, b2, w3, b3, w4, freq):
    w1p = jnp.pad(w1, ((0, 128 - HY_EMB), (0, 0)))
    row = lambda v: v.reshape(1, -1)
    hd = _hy_mlp(_hy_features(L), w1p, row(b1), w2, row(b2), w3, row(b3), row(freq))
    deltas = jnp.abs(jnp.linspace(math.log(HY_FAST) / HY_TARGET, math.log(HY_SLOW) / HY_TARGET, 2 * GROUP_W, dtype=F32))
    return _hy_filter(hd, w4, deltas.reshape(1, -1))


def _hyena(v, x1, x2, hy_params, hy_bias, *, L, Lc, want_ctx):
    R = v.shape[0]
    C = GROUP_W
    n2 = FFT_N2
    ncol = n2 * C
    tn = _tile(ncol, 8192, 128)
    bias_t = jnp.tile(hy_bias, (1, n2))

    n1 = 2 * L // n2
    kr = n1 // 2
    h_lat = _hy_filters(L, *hy_params)
    f1, g, ginv, f5 = _fft_tables(n1, kr, np.arange(n1 // 4, n1 // 4 + kr))
    h4 = _fft_forward(h_lat.reshape(kr, n2 * 2 * C), f1, g, k_rows=kr, n1=n1, C=2 * C)
    v2, x12, x22 = (t.reshape(R // n2, ncol) for t in (v, x1, x2))

    def lat_conv(u2d, h_cblk, gate_x, gate_v, bias_row, out_dtype):
        u4 = _fft_forward(u2d, f1, g, k_rows=kr, n1=n1, C=C)
        c4 = _fft_inner_mul(ginv, u4, h4, h_cblk)
        return _mm(f5, c4.reshape(2 * n1, ncol), tm=kr, tn=tn, out_dtype=out_dtype, epi=_gate_epi,
                   extras=((gate_x, "full"), (gate_v, "full"), (bias_row, "col")), name="fft_outer_inv")

    z1 = lat_conv(v2, 0, x12, v2, bias_t[0:1], F32)
    y_lat = lat_conv(z1, 1, x22, z1, bias_t[1:2], BF16).reshape(L, C)
    if not want_ctx:
        return jnp.concatenate([y_lat, jnp.zeros((Lc, C), BF16)], axis=0)

    assert Lc == n2
    n1c = 8
    krc = 16
    h_ctx = _hy_filters(Lc, *hy_params)
    f1c, gc, ginvc, f5c = _fft_tables(n1c, krc, np.arange(n1c))
    pad_rows = lambda t: jnp.pad(t.reshape(1, -1), ((0, krc - 1), (0, 0)))
    h4c = _fft_forward(pad_rows(h_ctx), f1c, gc, k_rows=krc, n1=n1c, C=2 * C)
    vc, x1c, x2c = (t[L:] for t in (v, x1, x2))

    def ctx_conv(u, h_cblk):
        u4 = _fft_forward(pad_rows(u), f1c, gc, k_rows=krc, n1=n1c, C=C)
        c4 = _fft_inner_mul(ginvc, u4, h4c, h_cblk)
        y = _mm(f5c, c4.reshape(2 * n1c, ncol), tm=n1c, tn=tn, out_dtype=F32, name="fft_outer_inv_ctx")
        return y.reshape(n1c * n2, C)[Lc // 2:Lc // 2 + Lc]

    z1c = _gate_mul(ctx_conv(vc, 0), x1c, vc, hy_bias[0:1], F32)
    y_ctx = _gate_mul(ctx_conv(z1c, 1), x2c, z1c, hy_bias[1:2], BF16)
    return jnp.concatenate([y_lat, y_ctx], axis=0)


def _gate_mul_kernel(y_ref, x_ref, v_ref, b_ref, o_ref):
    o_ref[...] = (x_ref[...] * (y_ref[...] + b_ref[...] * v_ref[...])).astype(o_ref.dtype)


def _gate_mul(y, x, v, bias_row, out_dtype):
    return pl.pallas_call(_gate_mul_kernel, out_shape=jax.ShapeDtypeStruct(y.shape, out_dtype),
                          name="hy_gate_ctx")(y, x, v, bias_row)


def kernel(x, c, ctx, c_ctx, norm1_w, norm2_w, mod_w, mod_b, w_in, w_out, gdn_conv_w, gdn_A_log, gdn_dt_bias, gdn_norm_w, mla_q_norm_w, mla_w_uq, mla_kv_norm_w, mla_w_ukv, hy_conv_w, hy_w1, hy_b1, hy_w2, hy_b2, hy_w3, hy_b3, hy_w4, hy_freq, hy_bias, ffn_w1, ffn_w3, ffn_w2, moe_router, moe_w1, moe_w3, moe_w2, final_norm_w):
    B, L, D = x.shape
    Lc = ctx.shape[1]
    depth = w_in.shape[0]
    assert B == 1 and D == 4 * GROUP_W and Lc == ROW_TILE and L % ROW_TILE == 0
    R = L + Lc
    n_lat_tiles = L // ROW_TILE
    row = lambda v: v.reshape(1, -1)
    pad_lanes = lambda v: jnp.pad(v.reshape(1, -1), ((0, 0), (0, 128 - v.size)))

    xa = jnp.concatenate([x[0], ctx[0]], axis=0)
    cc = jnp.pad(jnp.stack([c[0], c_ctx], axis=0), ((0, 6), (0, 0)))
    mla_cos, mla_sin, ret_cos, ret_sin = _rope_tables(L, Lc)
    ones_head = jnp.ones((1, HEAD_W), F32)
    tm_big = _tile(R, 1280)

    for l in range(depth):
        want_ctx = l < depth - 1
        mod = _mm(cc, mod_w[l], tm=8, tn=_tile(N_MOD * D, 1024, 128), out_dtype=F32, a_pro=_silu,
                  extras=((row(mod_b[l]), "col"),), epi=lambda acc, b, *, row0: acc + b, name="mod")
        mod3 = mod[0:2].reshape(2, 1, N_MOD * D)

        h = _norm_mod(xa, row(norm1_w[l]), mod3, 1, 0, n_lat_tiles=n_lat_tiles, out_dtype=BF16)
        P = _mm(h, _pack_w_in(w_in[l], l), tm=tm_big, tn=512, out_dtype=F32, name="w_in")

        gq, gk, gv = _conv3(P, gdn_conv_w[l], C_GQKV, n_lat_tiles=n_lat_tiles, gdn=True)
        o_f, o_b = _gdn_scan(gq, gk, gv, P, pad_lanes(gdn_A_log[l]), pad_lanes(gdn_dt_bias[l]), n_lat=L)
        gdn_o = _headnorm_gate(o_f, o_b, P, C_GZ, row(gdn_norm_w[l]))

        qw = jnp.pad(row(mla_q_norm_w[l]), ((0, 0), (0, 512 - MLA_Q_RANK)))
        qh, kh, vh = _mla_proj(P, mla_cos, mla_sin, qw, row(mla_kv_norm_w[l]), _pack_w_uq(mla_w_uq[l]),
                               mla_w_ukv[l].astype(BF16))
        mla_o = _flash(qh, kh, vh, n_lat=L, tq=ROW_TILE, tk=_tile(R, 1280))

        hv, hx1, hx2 = _conv3(P, hy_conv_w[l], C_HY, n_lat_tiles=n_lat_tiles, gdn=False)
        hy_params = (hy_w1[l], hy_b1[l], hy_w2[l], hy_b2[l], hy_w3[l], hy_b3[l], hy_w4[l], hy_freq[l])
        hy_o = _hyena(hv, hx1, hx2, hy_params, hy_bias[l], L=L, Lc=Lc, want_ctx=want_ctx)

        r_f, r_b = _ret_scan(P, ret_cos, ret_sin, n_lat=L)
        ret_o = _headnorm_gate(r_f, r_b, P, C_RG, ones_head)

        xa = _mix_out((gdn_o, mla_o, hy_o, ret_o), w_out[l].astype(BF16), xa, mod3, 2, n_lat=L)

        if l % 2 == 0:
            h2 = _norm_mod(xa, row(norm2_w[l]), mod3, 4, 3, n_lat_tiles=n_lat_tiles, out_dtype=BF16)
            u = _swiglu_up(h2, ffn_w1[l // 2].astype(BF16), ffn_w3[l // 2].astype(BF16))
            xa = _ffn_down(u, ffn_w2[l // 2].astype(BF16), xa, mod3, None, None, n_lat=L)
        else:
            router = jnp.pad(moe_router[l // 2], ((0, 0), (0, 128 - N_EXPERTS)))
            h2, gates = _norm_mod(xa, row(norm2_w[l]), mod3, 4, 3, n_lat_tiles=n_lat_tiles, out_dtype=BF16,
                                  router=router)
            for e in range(N_EXPERTS):
                u = _swiglu_up(h2, moe_w1[l // 2, e].astype(BF16), moe_w3[l // 2, e].astype(BF16))
                xa = _ffn_down(u, moe_w2[l // 2, e].astype(BF16), xa, mod3, gates, e, n_lat=L)

    zero_mod = jnp.zeros((2, 1, N_MOD * D), F32)
    out = _norm_mod(xa, row(final_norm_w), zero_mod, 0, 0, n_lat_tiles=n_lat_tiles, out_dtype=F32, rows=L)
    return out[None]


def _ffn_down(u, w2, xa, mod3, gates, e, *, n_lat):
    R, D = xa.shape
    tm = _tile(R, 640, 128)
    tn = 512
    g_lat = mod3[0, :, 5 * D:6 * D]
    g_ctx = mod3[1, :, 5 * D:6 * D]
    if gates is None:
        return _mm(u, w2, tm=tm, tn=tn, out_dtype=F32, name="ffn_down",
                   extras=((xa, "full"), (g_lat, "col"), (g_ctx, "col")),
                   epi=functools.partial(_resid_epi, n_lat=n_lat))
    return _mm(u, w2, tm=tm, tn=tn, out_dtype=F32, name="moe_down",
               extras=((xa, "full"), (g_lat, "col"), (g_ctx, "col"), (gates, "row")),
               epi=functools.partial(_resid_gate_epi, n_lat=n_lat, e=e))
```

```python
import functools
import math

import numpy as np
import jax
import jax.numpy as jnp
from jax import lax
from jax.experimental import pallas as pl
from jax.experimental.pallas import tpu as pltpu

F32 = jnp.float32
BF16 = jnp.bfloat16
HI = lax.Precision.HIGHEST

EPS = 1e-6
GRID_W = 64
N_MOD = 6
HEAD_W = 128
N_HEADS = 4
GROUP_W = N_HEADS * HEAD_W
GDN_CHUNK = 64
GDN_CONV = 3
MLA_NOPE = 128
MLA_ROPE = 64
MLA_Q_RANK = 448
MLA_KV_RANK = 128
ROPE_BASE = 10000.0
HY_EMB = 33
HY_FFN = 64
HY_FAST = 0.3
HY_SLOW = 1.5
HY_TARGET = 1e-2
RET_CHUNK = 128
RET_DECAY_F = (5.0, 7.0, 9.0, 11.0)
RET_DECAY_B = (6.0, 8.0, 10.0, 12.0)
N_EXPERTS = 8
ATTN_SUB = 256
FFT_N2 = 256
ROW_TILE = 256

C_GQKV = 0
C_HY = 1536
C_RQKV = 3072
C_GZ = 4608
C_RG = 5120
C_CQ = 5632
C_GAB = 6144
C_CKV = 6272
C_KR = 6400
P_COLS = 6656

VMEM_LIMIT = 56 * 1024 * 1024
NT_DIMS = (((1,), (1,)), ((), ()))
TN_DIMS = (((0,), (0,)), ((), ()))


def _cparams(sem):
    return pltpu.CompilerParams(dimension_semantics=sem, vmem_limit_bytes=VMEM_LIMIT)


def _tile(n, pref, unit=ROW_TILE):
    best = unit
    t = unit
    while t <= min(n, pref):
        if n % t == 0:
            best = t
        t += unit
    return best


def _silu(x):
    return x * jax.nn.sigmoid(x)


def _bdot(a, b):
    return jnp.dot(a.astype(BF16), b.astype(BF16), preferred_element_type=F32)


def _bdot_nt(a, b):
    return lax.dot_general(a.astype(BF16), b.astype(BF16), NT_DIMS, preferred_element_type=F32)


def _bdot_tn(a, b):
    return lax.dot_general(a.astype(BF16), b.astype(BF16), TN_DIMS, preferred_element_type=F32)


def _mm_kernel(*refs, n_extra, epi, a_pro, precision, tm, order):
    a_ref, b_ref = refs[0], refs[1]
    extras = refs[2:2 + n_extra]
    o_ref = refs[2 + n_extra]
    a = a_ref[...]
    if a_pro is not None:
        a = a_pro(a)
    if precision is None:
        acc = jnp.dot(a.astype(BF16), b_ref[...].astype(BF16), preferred_element_type=F32)
    else:
        acc = jnp.dot(a.astype(F32), b_ref[...].astype(F32), preferred_element_type=F32, precision=precision)
    if epi is not None:
        m = pl.program_id(0 if order == "mn" else 1)
        acc = epi(acc, *[e[...] for e in extras], row0=m * tm)
    o_ref[...] = acc.astype(o_ref.dtype)


def _mm(a, b, *, tm, tn, out_dtype, k=None, m=None, a_col_blk=0, order="mn", extras=(), epi=None,
        a_pro=None, precision=None, name="mm"):
    M = a.shape[0] if m is None else m
    K = a.shape[1] if k is None else k
    N = b.shape[1]
    assert M % tm == 0 and N % tn == 0, (M, tm, N, tn)
    gm, gn = M // tm, N // tn
    if order == "mn":
        grid = (gm, gn)
        mi = lambda i, j: i
        ni = lambda i, j: j
    else:
        grid = (gn, gm)
        mi = lambda i, j: j
        ni = lambda i, j: i
    in_specs = [pl.BlockSpec((tm, K), lambda i, j: (mi(i, j), a_col_blk)),
                pl.BlockSpec((K, tn), lambda i, j: (0, ni(i, j)))]
    ops = [a, b]
    for arr, kind in extras:
        if kind == "full":
            in_specs.append(pl.BlockSpec((tm, tn), lambda i, j: (mi(i, j), ni(i, j))))
        elif kind == "col":
            in_specs.append(pl.BlockSpec((arr.shape[0], tn), lambda i, j: (0, ni(i, j))))
        elif kind == "row":
            in_specs.append(pl.BlockSpec((tm, arr.shape[1]), lambda i, j: (mi(i, j), 0)))
        else:
            raise ValueError(kind)
        ops.append(arr)
    kern = functools.partial(_mm_kernel, n_extra=len(extras), epi=epi, a_pro=a_pro, precision=precision,
                             tm=tm, order=order)
    return pl.pallas_call(
        kern,
        out_shape=jax.ShapeDtypeStruct((M, N), out_dtype),
        grid=grid,
        in_specs=in_specs,
        out_specs=pl.BlockSpec((tm, tn), lambda i, j: (mi(i, j), ni(i, j))),
        compiler_params=_cparams(("parallel", "parallel")),
        name=name,
    )(*ops)


def _norm_mod_kernel(*refs, with_router, d):
    if with_router:
        x_ref, w_ref, sc_ref, sh_ref, r_ref, h_ref, g_ref = refs
    else:
        x_ref, w_ref, sc_ref, sh_ref, h_ref = refs
    x = x_ref[...]
    y = x * lax.rsqrt(jnp.sum(x * x, axis=-1, keepdims=True) * (1.0 / d) + EPS)
    h = (y * w_ref[...]) * (1.0 + sc_ref[...]) + sh_ref[...]
    h_ref[...] = h.astype(h_ref.dtype)
    if with_router:
        logits = jnp.dot(h, r_ref[...], preferred_element_type=F32, precision=HI)
        lane = lax.broadcasted_iota(jnp.int32, logits.shape, 1)
        neg = jnp.float32(-jnp.inf)
        lg = jnp.where(lane < N_EXPERTS, logits, neg)
        m1 = jnp.max(lg, axis=-1, keepdims=True)
        i1 = jnp.min(jnp.where(lg == m1, lane, 128), axis=-1, keepdims=True)
        lg2 = jnp.where(lane == i1, neg, lg)
        m2 = jnp.max(lg2, axis=-1, keepdims=True)
        i2 = jnp.min(jnp.where(lg2 == m2, lane, 128), axis=-1, keepdims=True)
        e2 = jnp.exp(m2 - m1)
        den = 1.0 + e2
        g_ref[...] = (jnp.where(lane == 8, i1.astype(F32), 0.0) + jnp.where(lane == 9, i2.astype(F32), 0.0)
                      + jnp.where(lane == 10, 1.0 / den, 0.0) + jnp.where(lane == 11, e2 / den, 0.0))


def _norm_mod(x, w, mod3, sc_blk, sh_blk, *, n_lat_tiles, out_dtype, router=None, rows=None):
    R = x.shape[0] if rows is None else rows
    D = x.shape[1]
    T = ROW_TILE
    which = lambda i: jnp.where(i >= n_lat_tiles, 1, 0)
    in_specs = [pl.BlockSpec((T, D), lambda i: (i, 0)),
                pl.BlockSpec((1, D), lambda i: (0, 0)),
                pl.BlockSpec((None, 1, D), lambda i: (which(i), 0, sc_blk)),
                pl.BlockSpec((None, 1, D), lambda i: (which(i), 0, sh_blk))]
    ops = [x, w, mod3, mod3]
    out_shape = [jax.ShapeDtypeStruct((R, D), out_dtype)]
    out_specs = [pl.BlockSpec((T, D), lambda i: (i, 0))]
    if router is not None:
        in_specs.append(pl.BlockSpec((D, 128), lambda i: (0, 0)))
        ops.append(router)
        out_shape.append(jax.ShapeDtypeStruct((R, 128), F32))
        out_specs.append(pl.BlockSpec((T, 128), lambda i: (i, 0)))
    res = pl.pallas_call(
        functools.partial(_norm_mod_kernel, with_router=router is not None, d=D),
        out_shape=out_shape, grid=(R // T,), in_specs=in_specs, out_specs=out_specs,
        compiler_params=_cparams(("parallel",)), name="norm_mod",
    )(*ops)
    return res if router is not None else res[0]


def _conv3_kernel(x_ref, prev_ref, next_ref, w_ref, o0, o1, o2, *, n_lat_tiles, gdn):
    i = pl.program_id(0)
    x = x_ref[...]
    T = x.shape[0]
    row = lax.broadcasted_iota(jnp.int32, x.shape, 0)
    has_prev = jnp.logical_and(i != 0, i != n_lat_tiles)
    has_next = jnp.logical_and(i != n_lat_tiles - 1, i != n_lat_tiles)
    prev_row = jnp.where(has_prev, prev_ref[7:8, :], 0.0)
    next_row = jnp.where(has_next, next_ref[0:1, :], 0.0)
    xm = jnp.where(row == 0, prev_row, pltpu.roll(x, 1, 0))
    xp = jnp.where(row == T - 1, next_row, pltpu.roll(x, T - 1, 0))
    w = w_ref[...]
    y = xm * w[0:1, :] + x * w[1:2, :] + xp * w[2:3, :]
    outs = (o0, o1, o2)
    if not gdn:
        for p in range(3):
            outs[p][...] = y[:, p * GROUP_W:(p + 1) * GROUP_W]
        return
    y = _silu(y)
    for p in range(3):
        for h in range(N_HEADS):
            c0 = p * GROUP_W + h * HEAD_W
            t = y[:, c0:c0 + HEAD_W]
            if p < 2:
                t = t * lax.rsqrt(jnp.sum(t * t, axis=-1, keepdims=True) + EPS)
                if p == 0:
                    t = t * (HEAD_W ** -0.5)
            outs[p][:, h * HEAD_W:(h + 1) * HEAD_W] = t


def _conv3(P, w, col0, *, n_lat_tiles, gdn):
    R = P.shape[0]
    T = ROW_TILE
    W = 3 * GROUP_W
    cb = col0 // W
    assert col0 % W == 0
    nb8 = R // 8
    tpb = T // 8
    in_specs = [pl.BlockSpec((T, W), lambda i: (i, cb)),
                pl.BlockSpec((8, W), lambda i: (jnp.maximum(i * tpb - 1, 0), cb)),
                pl.BlockSpec((8, W), lambda i: (jnp.minimum((i + 1) * tpb, nb8 - 1), cb)),
                pl.BlockSpec((GDN_CONV, W), lambda i: (0, 0))]
    return pl.pallas_call(
        functools.partial(_conv3_kernel, n_lat_tiles=n_lat_tiles, gdn=gdn),
        out_shape=[jax.ShapeDtypeStruct((R, GROUP_W), F32)] * 3,
        grid=(R // T,), in_specs=in_specs,
        out_specs=[pl.BlockSpec((T, GROUP_W), lambda i: (i, 0))] * 3,
        compiler_params=_cparams(("parallel",)), name="conv3_gdn" if gdn else "conv3_hy",
    )(P, P, P, w)


def _gdn_kernel(qf, kf, vf, gf, qb, kb_, vb, gb, alog_ref, dtb_ref, of_ref, ob_ref, s_scr):
    C = GDN_CHUNK

    @pl.when(pl.program_id(0) == 0)
    def _():
        s_scr[...] = jnp.zeros_like(s_scr)

    r = lax.broadcasted_iota(jnp.int32, (C, C), 0)
    c = lax.broadcasted_iota(jnp.int32, (C, C), 1)
    eye = (r == c).astype(F32)
    lane = lax.broadcasted_iota(jnp.int32, (C, 128), 1)
    alog = alog_ref[...]
    dtb = dtb_ref[...]
    dirs = ((qf, kf, vf, gf, of_ref, r >= c, r > c), (qb, kb_, vb, gb, ob_ref, r <= c, r < c))
    for d, (q_ref, k_ref, v_ref, g_ref, o_ref, incl, strict) in enumerate(dirs):
        x = g_ref[...]
        t = x + dtb
        softplus = jnp.maximum(t, 0.0) + jnp.log1p(jnp.exp(-jnp.abs(t)))
        g_all = -jnp.exp(alog) * softplus
        beta_all = jax.nn.sigmoid(x)
        cs_all = jnp.dot(incl.astype(F32), g_all, preferred_element_type=F32, precision=HI)
        tot_all = jnp.sum(g_all, axis=0, keepdims=True)
        for h in range(N_HEADS):
            l = N_HEADS * d + h
            gcol = cs_all[:, l:l + 1]
            bcol = beta_all[:, 8 + l:9 + l]
            tot = tot_all[:, l:l + 1]
            q = q_ref[:, h * HEAD_W:(h + 1) * HEAD_W]
            k = k_ref[:, h * HEAD_W:(h + 1) * HEAD_W]
            v = v_ref[:, h * HEAD_W:(h + 1) * HEAD_W]
            u_mat = jnp.where(lane == 0, gcol, jnp.where(lane == 1, 1.0, 0.0))
            v_mat = jnp.where(lane == 0, 1.0, jnp.where(lane == 1, -gcol, 0.0))
            diff = lax.dot_general(u_mat, v_mat, NT_DIMS, preferred_element_type=F32, precision=HI)
            decay = jnp.exp(jnp.where(incl, diff, -1e30))
            kbeta = k * bcol
            a_mat = jnp.where(strict, _bdot_nt(kbeta, k) * decay, 0.0)
            n_mat = -a_mat
            t_mat = eye + n_mat
            p_mat = n_mat
            for _ in range(int(math.log2(C)) - 1):
                p_mat = _bdot(p_mat, p_mat)
                t_mat = t_mat + _bdot(t_mat, p_mat)
            u = _bdot(t_mat, v * bcol)
            w = _bdot(t_mat, kbeta * jnp.exp(gcol))
            s = s_scr[l]
            v_new = u - _bdot(w, s)
            attn = _bdot_nt(q, k) * decay
            o = _bdot(q * jnp.exp(gcol), s) + _bdot(attn, v_new)
            s_scr[l] = s * jnp.exp(tot) + _bdot_tn(k * jnp.exp(tot - gcol), v_new)
            o_ref[:, h * HEAD_W:(h + 1) * HEAD_W] = o


def _gdn_scan(q, k, v, P, alog_row, dtb_row, *, n_lat):
    R = q.shape[0]
    C = GDN_CHUNK
    nt = R // C
    nl = n_lat // C
    gcb = C_GAB // 128
    fwd = lambda i: ((i + nl) % nt, 0)
    bwd = lambda i: (nt - 1 - i, 0)
    gfwd = lambda i: ((i + nl) % nt, gcb)
    gbwd = lambda i: (nt - 1 - i, gcb)
    blk = lambda im: pl.BlockSpec((C, GROUP_W), im)
    in_specs = [blk(fwd), blk(fwd), blk(fwd), pl.BlockSpec((C, 128), gfwd),
                blk(bwd), blk(bwd), blk(bwd), pl.BlockSpec((C, 128), gbwd),
                pl.BlockSpec((1, 128), lambda i: (0, 0)), pl.BlockSpec((1, 128), lambda i: (0, 0))]
    return pl.pallas_call(
        _gdn_kernel,
        out_shape=[jax.ShapeDtypeStruct((R, GROUP_W), F32)] * 2,
        grid=(nt,), in_specs=in_specs, out_specs=[blk(fwd), blk(bwd)],
        scratch_shapes=[pltpu.VMEM((2 * N_HEADS, HEAD_W, HEAD_W), F32)],
        compiler_params=_cparams(("arbitrary",)), name="gdn_scan",
    )(q, k, v, P, q, k, v, P, alog_row, dtb_row)


def _ret_kernel(qf, kf, vf, cf, sf, qb, kb_, vb, cb, sb, dmat_ref, qdec_ref, kdec_ref, cdec_ref,
                of_ref, ob_ref, s_scr):
    @pl.when(pl.program_id(0) == 0)
    def _():
        s_scr[...] = jnp.zeros_like(s_scr)

    dirs = ((qf, kf, vf, cf, sf, of_ref), (qb, kb_, vb, cb, sb, ob_ref))
    for d, (q_ref, k_ref, v_ref, cos_ref, sin_ref, o_ref) in enumerate(dirs):
        cos = cos_ref[...]
        sin = sin_ref[...]
        for h in range(N_HEADS):
            l = N_HEADS * d + h
            sl = slice(h * HEAD_W, (h + 1) * HEAD_W)
            q = q_ref[:, sl]
            k = k_ref[:, sl]
            v = v_ref[:, sl]
            q = q * cos + pltpu.roll(q, HEAD_W // 2, 1) * sin
            k = (k * cos + pltpu.roll(k, HEAD_W // 2, 1) * sin) * (HEAD_W ** -0.5)
            s = s_scr[l]
            inner = _bdot_nt(q, k) * dmat_ref[l]
            o = _bdot(q, s) * qdec_ref[l] + _bdot(inner, v)
            s_scr[l] = s * cdec_ref[l] + _bdot_tn(k * kdec_ref[l], v)
            o_ref[:, sl] = o


def _ret_tables():
    C = RET_CHUNK
    lg = np.concatenate([np.log1p(-np.exp2(-np.asarray(RET_DECAY_F, np.float32)).astype(np.float32)),
                         np.log1p(-np.exp2(-np.asarray(RET_DECAY_B, np.float32)).astype(np.float32))]).astype(np.float64)
    idx = np.arange(C, dtype=np.float64)
    rel = idx[:, None] - idx[None, :]
    dmat = np.zeros((8, C, C))
    qdec = np.zeros((8, C, HEAD_W))
    kdec = np.zeros((8, C, HEAD_W))
    cdec = np.zeros((8, C, HEAD_W))
    for l in range(8):
        low = np.where(rel >= 0, np.exp(np.maximum(rel, 0.0) * lg[l]), 0.0)
        if l < N_HEADS:
            dmat[l] = low
            qdec[l] = np.exp((idx + 1.0) * lg[l])[:, None]
            kdec[l] = np.exp((C - 1.0 - idx) * lg[l])[:, None]
        else:
            dmat[l] = low.T
            qdec[l] = np.exp((C - idx) * lg[l])[:, None]
            kdec[l] = np.exp(idx * lg[l])[:, None]
        cdec[l] = np.exp(C * lg[l])
    return tuple(jnp.asarray(t, F32) for t in (dmat, qdec, kdec, cdec))


def _ret_scan(P, cos_t, sin_t, *, n_lat):
    R = P.shape[0]
    C = RET_CHUNK
    nt = R // C
    nl = n_lat // C
    cb0 = C_RQKV // GROUP_W
    fi = lambda i: (i + nl) % nt
    bi = lambda i: nt - 1 - i
    pblk = lambda f, c: pl.BlockSpec((C, GROUP_W), lambda i: (f(i), c))
    tblk = lambda f: pl.BlockSpec((C, HEAD_W), lambda i: (f(i), 0))
    cst = lambda: pl.BlockSpec((8, C, HEAD_W), lambda i: (0, 0, 0))
    in_specs = [pblk(fi, cb0), pblk(fi, cb0 + 1), pblk(fi, cb0 + 2), tblk(fi), tblk(fi),
                pblk(bi, cb0), pblk(bi, cb0 + 1), pblk(bi, cb0 + 2), tblk(bi), tblk(bi),
                cst(), cst(), cst(), cst()]
    dmat, qdec, kdec, cdec = _ret_tables()
    oblk = lambda f: pl.BlockSpec((C, GROUP_W), lambda i: (f(i), 0))
    return pl.pallas_call(
        _ret_kernel,
        out_shape=[jax.ShapeDtypeStruct((R, GROUP_W), F32)] * 2,
        grid=(nt,), in_specs=in_specs, out_specs=[oblk(fi), oblk(bi)],
        scratch_shapes=[pltpu.VMEM((2 * N_HEADS, HEAD_W, HEAD_W), F32)],
        compiler_params=_cparams(("arbitrary",)), name="ret_scan",
    )(P, P, P, cos_t, sin_t, P, P, P, cos_t, sin_t, dmat, qdec, kdec, cdec)


def _headnorm_kernel(a_ref, b_ref, g_ref, w_ref, o_ref):
    w = w_ref[...]
    for h in range(N_HEADS):
        sl = slice(h * HEAD_W, (h + 1) * HEAD_W)
        o = a_ref[:, sl] + b_ref[:, sl]
        y = o * lax.rsqrt(jnp.sum(o * o, axis=-1, keepdims=True) * (1.0 / HEAD_W) + EPS)
        o_ref[:, sl] = ((y * w) * _silu(g_ref[:, sl])).astype(o_ref.dtype)


def _headnorm_gate(a, b, P, gate_col0, w_row):
    R = a.shape[0]
    T = ROW_TILE
    gb = gate_col0 // GROUP_W
    return pl.pallas_call(
        _headnorm_kernel,
        out_shape=jax.ShapeDtypeStruct((R, GROUP_W), BF16),
        grid=(R // T,),
        in_specs=[pl.BlockSpec((T, GROUP_W), lambda i: (i, 0)), pl.BlockSpec((T, GROUP_W), lambda i: (i, 0)),
                  pl.BlockSpec((T, GROUP_W), lambda i: (i, gb)), pl.BlockSpec((1, HEAD_W), lambda i: (0, 0))],
        out_specs=pl.BlockSpec((T, GROUP_W), lambda i: (i, 0)),
        compiler_params=_cparams(("parallel",)), name="headnorm_gate",
    )(a, b, P, w_row)


def _mla_proj_kernel(cq_ref, ckv_ref, kr_ref, cos_ref, sin_ref, qw_ref, kvw_ref, wuq_ref, wukv_ref,
                     q_out, k_out, v_out):
    cq = cq_ref[...]
    nq = cq * lax.rsqrt(jnp.sum(cq * cq, axis=-1, keepdims=True) * (1.0 / MLA_Q_RANK) + EPS) * qw_ref[...]
    ckv = ckv_ref[...]
    nkv = ckv * lax.rsqrt(jnp.sum(ckv * ckv, axis=-1, keepdims=True) * (1.0 / MLA_KV_RANK) + EPS) * kvw_ref[...]
    q = _bdot(nq, wuq_ref[...])
    kv = _bdot(nkv, wukv_ref[...])
    cos = cos_ref[...]
    sin = sin_ref[...]
    kr = kr_ref[...]
    kr = kr * cos + pltpu.roll(kr, 64, 1) * sin
    scale = (MLA_NOPE + MLA_ROPE) ** -0.5 * math.log2(math.e)
    ones = jnp.ones((cq.shape[0], 128), v_out.dtype)
    for h in range(N_HEADS):
        q_out[:, 256 * h:256 * h + 128] = (q[:, 256 * h:256 * h + 128] * scale).astype(q_out.dtype)
        qr = q[:, 256 * h + 128:256 * h + 256]
        qr = qr * cos + pltpu.roll(qr, 64, 1) * sin
        q_out[:, 256 * h + 128:256 * h + 256] = (qr * scale).astype(q_out.dtype)
        k_out[:, 256 * h:256 * h + 128] = kv[:, 256 * h:256 * h + 128].astype(k_out.dtype)
        k_out[:, 256 * h + 128:256 * h + 256] = kr.astype(k_out.dtype)
        v_out[:, 256 * h:256 * h + 128] = kv[:, 256 * h + 128:256 * h + 256].astype(v_out.dtype)
        v_out[:, 256 * h + 128:256 * h + 256] = ones


def _mla_proj(P, cos_t, sin_t, qw_row, kvw_row, wuq_ext, wukv):
    R = P.shape[0]
    T = ROW_TILE
    cst = lambda s: pl.BlockSpec(s, lambda i: (0, 0))
    return pl.pallas_call(
        _mla_proj_kernel,
        out_shape=[jax.ShapeDtypeStruct((R, 1024), BF16)] * 3,
        grid=(R // T,),
        in_specs=[pl.BlockSpec((T, 512), lambda i: (i, C_CQ // 512)),
                  pl.BlockSpec((T, 128), lambda i: (i, C_CKV // 128)),
                  pl.BlockSpec((T, 128), lambda i: (i, C_KR // 128)),
                  pl.BlockSpec((T, 128), lambda i: (i, 0)), pl.BlockSpec((T, 128), lambda i: (i, 0)),
                  cst((1, 512)), cst((1, 128)), cst((512, 1024)), cst((128, 1024))],
        out_specs=[pl.BlockSpec((T, 1024), lambda i: (i, 0))] * 3,
        compiler_params=_cparams(("parallel",)), name="mla_proj",
    )(P, P, P, cos_t, sin_t, qw_row, kvw_row, wuq_ext, wukv)


def _flash_kernel(q_ref, k_ref, v_ref, o_ref, m_scr, acc_scr, *, nk):
    j = pl.program_id(2)

    @pl.when(j == 0)
    def _():
        m_scr[...] = jnp.full_like(m_scr, -jnp.inf)
        acc_scr[...] = jnp.zeros_like(acc_scr)

    q = q_ref[...]
    m = m_scr[...]
    acc = acc_scr[...]
    for c0 in range(0, k_ref.shape[0], ATTN_SUB):
        s = lax.dot_general(q, k_ref[c0:c0 + ATTN_SUB, :], NT_DIMS, preferred_element_type=F32)
        m_new = jnp.maximum(m, jnp.max(jnp.maximum(s[:, 0:128], s[:, 128:256]), axis=-1, keepdims=True))
        alpha = jnp.exp2(m - m_new)
        p = jnp.exp2(s - m_new).astype(BF16)
        acc = alpha * acc + jnp.dot(p, v_ref[c0:c0 + ATTN_SUB, :], preferred_element_type=F32)
        m = m_new
    acc_scr[...] = acc
    m_scr[...] = m

    @pl.when(j == nk - 1)
    def _():
        acc = acc_scr[...]
        o_ref[...] = (acc[:, 0:HEAD_W] / acc[:, HEAD_W:HEAD_W + 1]).astype(o_ref.dtype)


def _flash(qh, kh, vh, *, q_row0, n_q, k_row0, n_k, tq, tk):
    assert n_q % tq == 0 and n_k % tk == 0 and q_row0 % tq == 0 and k_row0 % tk == 0
    nq, nk = n_q // tq, n_k // tk
    qb, kb = q_row0 // tq, k_row0 // tk
    return pl.pallas_call(
        functools.partial(_flash_kernel, nk=nk),
        out_shape=jax.ShapeDtypeStruct((n_q, GROUP_W), BF16),
        grid=(N_HEADS, nq, nk),
        in_specs=[pl.BlockSpec((tq, 256), lambda h, i, j: (qb + i, h)),
                  pl.BlockSpec((tk, 256), lambda h, i, j: (kb + j, h)),
                  pl.BlockSpec((tk, 256), lambda h, i, j: (kb + j, h))],
        out_specs=pl.BlockSpec((tq, HEAD_W), lambda h, i, j: (i, h)),
        scratch_shapes=[pltpu.VMEM((tq, 1), F32), pltpu.VMEM((tq, 2 * HEAD_W), F32)],
        compiler_params=_cparams(("parallel", "parallel", "arbitrary")), name="mla_attention",
    )(qh, kh, vh)


def _hy_mlp_kernel(z_ref, w1, b1, w2, b2, w3, b3, fr, o_ref):
    f = fr[...]
    hdot = lambda a, b: jnp.dot(a, b, preferred_element_type=F32, precision=HI)
    h = jnp.sin(f * (hdot(z_ref[...], w1[...]) + b1[...]))
    h = jnp.sin(f * (hdot(h, w2[...]) + b2[...]))
    o_ref[...] = jnp.sin(f * (hdot(h, w3[...]) + b3[...]))


def _hy_mlp(z, w1p, b1, w2, b2, w3, b3, fr):
    L = z.shape[0]
    T = _tile(L, 2048)
    cst = lambda s: pl.BlockSpec(s, lambda i: (0, 0))
    return pl.pallas_call(
        _hy_mlp_kernel, out_shape=jax.ShapeDtypeStruct((L, HY_FFN), F32), grid=(L // T,),
        in_specs=[pl.BlockSpec((T, 128), lambda i: (i, 0)), cst((128, HY_FFN)), cst((1, HY_FFN)),
                  cst((HY_FFN, HY_FFN)), cst((1, HY_FFN)), cst((HY_FFN, HY_FFN)), cst((1, HY_FFN)),
                  cst((1, HY_FFN))],
        out_specs=pl.BlockSpec((T, HY_FFN), lambda i: (i, 0)),
        compiler_params=_cparams(("parallel",)), name="hy_filter_mlp",
    )(z, w1p, b1, w2, b2, w3, b3, fr)


def _hy_filter_kernel(hd_ref, w4_ref, dl_ref, o_ref, sum_scr, *, L, T):
    ph = pl.program_id(1)
    r = pl.program_id(2)
    h = jnp.dot(hd_ref[...], w4_ref[...], preferred_element_type=F32, precision=HI)
    p = (r * T + lax.broadcasted_iota(jnp.int32, (T, 1), 0)).astype(F32)
    dist = jnp.abs(p - (L // 2)) / L
    h = h * jnp.exp(-dist * dl_ref[...])

    @pl.when(jnp.logical_and(ph == 0, r == 0))
    def _():
        sum_scr[...] = jnp.zeros_like(sum_scr)

    @pl.when(ph == 0)
    def _():
        sum_scr[...] += jnp.sum(jnp.abs(h), axis=0, keepdims=True)

    @pl.when(ph == 1)
    def _():
        o_ref[...] = h / (sum_scr[...] + EPS)


def _hy_filter(hd, w4, deltas_row):
    L = hd.shape[0]
    C2 = w4.shape[1]
    T = _tile(L, 2048)
    tc = 256
    return pl.pallas_call(
        functools.partial(_hy_filter_kernel, L=L, T=T),
        out_shape=jax.ShapeDtypeStruct((L, C2), F32),
        grid=(C2 // tc, 2, L // T),
        in_specs=[pl.BlockSpec((T, HY_FFN), lambda c, p, r: (r, 0)),
                  pl.BlockSpec((HY_FFN, tc), lambda c, p, r: (0, c)),
                  pl.BlockSpec((1, tc), lambda c, p, r: (0, c))],
        out_specs=pl.BlockSpec((T, tc), lambda c, p, r: (r * p, c)),
        scratch_shapes=[pltpu.VMEM((1, tc), F32)],
        compiler_params=_cparams(("arbitrary", "arbitrary", "arbitrary")), name="hy_filter",
    )(hd, w4, deltas_row)


def _cplx_dot(g_ref, pr, pi, o_ref):
    gr = g_ref[0]
    gi = g_ref[1]
    dot = lambda a, b: jnp.dot(a, b, preferred_element_type=F32)
    o_ref[0] = (dot(gr, pr) - dot(gi, pi)).astype(o_ref.dtype)
    o_ref[1] = (dot(gi, pr) + dot(gr, pi)).astype(o_ref.dtype)


def _fft_inner_kernel(g_ref, a_ref, b_ref):
    _cplx_dot(g_ref, a_ref[0], a_ref[1], b_ref)


def _fft_inner_mul_kernel(g_ref, u_ref, h_ref, c_ref):
    ur = u_ref[0].astype(F32)
    ui = u_ref[1].astype(F32)
    hr = h_ref[0].astype(F32)
    hi = h_ref[1].astype(F32)
    pr = (ur * hr - ui * hi).astype(BF16)
    pi = (ur * hi + ui * hr).astype(BF16)
    _cplx_dot(g_ref, pr, pi, c_ref)


def _fft_inner(g, a4, *, tc=GROUP_W):
    _, n1, n2, C = a4.shape
    blk = pl.BlockSpec((2, None, n2, tc), lambda k, c: (0, k, 0, c))
    return pl.pallas_call(
        _fft_inner_kernel,
        out_shape=jax.ShapeDtypeStruct((2, n1, n2, C), BF16),
        grid=(n1, C // tc),
        in_specs=[pl.BlockSpec((None, 2, n2, n2), lambda k, c: (k, 0, 0, 0)), blk],
        out_specs=blk,
        compiler_params=_cparams(("parallel", "parallel")), name="fft_inner",
    )(g, a4)


def _fft_inner_mul(ginv, u4, h4, h_cblk, *, tc=GROUP_W):
    _, n1, n2, C = u4.shape
    assert C == tc
    ublk = pl.BlockSpec((2, None, n2, tc), lambda k: (0, k, 0, 0))
    hblk = pl.BlockSpec((2, None, n2, tc), lambda k: (0, k, 0, h_cblk))
    return pl.pallas_call(
        _fft_inner_mul_kernel,
        out_shape=jax.ShapeDtypeStruct((2, n1, n2, C), BF16),
        grid=(n1,),
        in_specs=[pl.BlockSpec((None, 2, n2, n2), lambda k: (k, 0, 0, 0)), ublk, hblk],
        out_specs=ublk,
        compiler_params=_cparams(("parallel",)), name="fft_inner_mul",
    )(ginv, u4, h4)


def _fft_tables(n1, k_rows, out_rows):
    n2 = FFT_N2
    N = n1 * n2
    k1 = np.arange(n1)
    ang1 = 2.0 * np.pi * ((k1[:, None] * np.arange(k_rows)[None, :]) % n1) / n1
    f1 = np.concatenate([np.cos(ang1), -np.sin(ang1)], axis=0)
    orow = np.asarray(out_rows)
    ang5 = 2.0 * np.pi * ((orow[:, None] * k1[None, :]) % n1) / n1
    f5 = np.concatenate([np.cos(ang5), -np.sin(ang5)], axis=1) / N
    i2 = jnp.arange(n2, dtype=jnp.int32)
    a1 = (2.0 * math.pi / N) * (jnp.arange(n1, dtype=jnp.int32)[:, None] * i2[None, :]).astype(F32)
    a2 = (2.0 * math.pi / n2) * ((i2[:, None] * i2[None, :]) % n2).astype(F32)
    c1, s1, c2, s2 = jnp.cos(a1), jnp.sin(a1), jnp.cos(a2), jnp.sin(a2)
    gr = c1[:, None, :] * c2[None, :, :] - s1[:, None, :] * s2[None, :, :]
    gi = -(s1[:, None, :] * c2[None, :, :] + c1[:, None, :] * s2[None, :, :])
    g = jnp.stack([gr, gi], axis=1).astype(BF16)
    vr = c1[:, :, None] * c2[None, :, :] - s1[:, :, None] * s2[None, :, :]
    vi = s1[:, :, None] * c2[None, :, :] + c1[:, :, None] * s2[None, :, :]
    ginv = jnp.stack([vr, vi], axis=1).astype(BF16)
    return jnp.asarray(f1, BF16), g, ginv, jnp.asarray(f5, BF16)


def _fft_forward(x2d, f1, g, *, k_rows, n1, C, tc=GROUP_W):
    ncol = FFT_N2 * C
    a = _mm(f1, x2d, tm=2 * n1, tn=_tile(ncol, 8192, 128), out_dtype=BF16, k=k_rows, name="fft_outer_fwd")
    return _fft_inner(g, a.reshape(2, n1, FFT_N2, C), tc=tc)


def _gate_epi(acc, x, v, bias, *, row0):
    return x * (acc + bias * v)


def _resid_epi(acc, x, g_lat, g_ctx, *, row0, n_lat):
    row = row0 + lax.broadcasted_iota(jnp.int32, (acc.shape[0], 1), 0)
    return x + jnp.where(row < n_lat, g_lat, g_ctx) * acc


def _mix_out_kernel(a0, a1, a2, a3, w_ref, x_ref, gl_ref, gc_ref, o_ref, *, tm, n_lat):
    acc = jnp.dot(a0[...], w_ref[0:GROUP_W, :], preferred_element_type=F32)
    for g, a in ((1, a1), (2, a2), (3, a3)):
        acc = acc + jnp.dot(a[...], w_ref[g * GROUP_W:(g + 1) * GROUP_W, :], preferred_element_type=F32)
    o_ref[...] = _resid_epi(acc, x_ref[...], gl_ref[...], gc_ref[...], row0=pl.program_id(0) * tm, n_lat=n_lat)


def _mix_out(parts, w_out_b, x, mod3, g_blk, *, n_lat):
    R, D = x.shape
    tm = _tile(R, 1280)
    tn = _tile(D, 1024, 128)
    ablk = pl.BlockSpec((tm, GROUP_W), lambda i, j: (i, 0))
    gspec = lambda which: pl.BlockSpec((None, 1, tn), lambda i, j: (which, 0, g_blk * (D // tn) + j))
    return pl.pallas_call(
        functools.partial(_mix_out_kernel, tm=tm, n_lat=n_lat),
        out_shape=jax.ShapeDtypeStruct((R, D), F32),
        grid=(R // tm, D // tn),
        in_specs=[ablk, ablk, ablk, ablk, pl.BlockSpec((4 * GROUP_W, tn), lambda i, j: (0, j)),
                  pl.BlockSpec((tm, tn), lambda i, j: (i, j)), gspec(0), gspec(1)],
        out_specs=pl.BlockSpec((tm, tn), lambda i, j: (i, j)),
        compiler_params=_cparams(("parallel", "parallel")), name="mix_out",
    )(*parts, w_out_b, x, mod3, mod3)


def _swiglu_up_kernel(a_ref, w1_ref, w3_ref, o_ref):
    a = a_ref[...]
    u = jnp.dot(a, w1_ref[...], preferred_element_type=F32)
    g = jnp.dot(a, w3_ref[...], preferred_element_type=F32)
    o_ref[...] = (_silu(u) * g).astype(o_ref.dtype)


def _swiglu_up(h, w1, w3):
    R, D = h.shape
    F = w1.shape[1]
    tm = _tile(R, 1280)
    tn = _tile(F, 512, 128)
    return pl.pallas_call(
        _swiglu_up_kernel, out_shape=jax.ShapeDtypeStruct((R, F), BF16), grid=(R // tm, F // tn),
        in_specs=[pl.BlockSpec((tm, D), lambda i, j: (i, 0)), pl.BlockSpec((D, tn), lambda i, j: (0, j)),
                  pl.BlockSpec((D, tn), lambda i, j: (0, j))],
        out_specs=pl.BlockSpec((tm, tn), lambda i, j: (i, j)),
        compiler_params=_cparams(("parallel", "parallel")), name="swiglu_up",
    )(h, w1, w3)


MOE_TILE = 512
GATHER_TILE = 256


def _gather_kernel(idx_ref, src_ref, *rest, n_lists, n_extra, T, M, epi):
    extras = rest[:n_extra]
    o_ref, buf, sem = rest[n_extra:n_extra + 3]
    i = pl.program_id(0)

    def row_copy(l, r, tok):
        return pltpu.make_async_copy(src_ref.at[pl.ds(tok, 1), :], buf.at[l, pl.ds(r, 1), :], sem.at[l])

    for l in range(n_lists):
        def issue(r, carry, l=l):
            row_copy(l, r, idx_ref[l * M + i * T + r]).start()
            return carry
        lax.fori_loop(0, T, issue, 0, unroll=8)
    for l in range(n_lists):
        pltpu.make_async_copy(src_ref.at[pl.ds(0, T), :], buf.at[l], sem.at[l]).wait()
    vals = [buf[l] for l in range(n_lists)]
    o_ref[...] = epi(*vals, *[e[...] for e in extras], row0=i * T).astype(o_ref.dtype)


def _row_gather(src, idx, *, n_lists, out_dtype, epi, extras=(), name):
    D = src.shape[1]
    M = idx.shape[0] // n_lists
    T = GATHER_TILE
    assert M % T == 0
    in_specs = [pl.BlockSpec(memory_space=pl.ANY)]
    ops = [src]
    for arr, kind in extras:
        if kind == "tile":
            in_specs.append(pl.BlockSpec((T, arr.shape[1]), lambda i, idx_ref: (i, 0)))
        else:
            in_specs.append(pl.BlockSpec(arr.shape, lambda i, idx_ref: (0, 0)))
        ops.append(arr)
    grid_spec = pltpu.PrefetchScalarGridSpec(
        num_scalar_prefetch=1, grid=(M // T,), in_specs=in_specs,
        out_specs=pl.BlockSpec((T, D), lambda i, idx_ref: (i, 0)),
        scratch_shapes=[pltpu.VMEM((n_lists, T, D), src.dtype), pltpu.SemaphoreType.DMA((n_lists,))])
    return pl.pallas_call(
        functools.partial(_gather_kernel, n_lists=n_lists, n_extra=len(extras), T=T, M=M, epi=epi),
        out_shape=jax.ShapeDtypeStruct((M, D), out_dtype), grid_spec=grid_spec,
        compiler_params=_cparams(("arbitrary",)), name=name,
    )(idx, *ops)


def _moe_up_kernel(te_ref, nv_ref, x_ref, w1_ref, w3_ref, o_ref):
    t = pl.program_id(1)

    @pl.when(t < nv_ref[0])
    def _():
        a = x_ref[...]
        u = jnp.dot(a, w1_ref[...], preferred_element_type=F32)
        g = jnp.dot(a, w3_ref[...], preferred_element_type=F32)
        o_ref[...] = (_silu(u) * g).astype(o_ref.dtype)

    @pl.when(t >= nv_ref[0])
    def _():
        o_ref[...] = jnp.zeros_like(o_ref)


def _moe_down_kernel(te_ref, nv_ref, u_ref, w2_ref, o_ref):
    t = pl.program_id(1)

    @pl.when(t < nv_ref[0])
    def _():
        o_ref[...] = jnp.dot(u_ref[...], w2_ref[...], preferred_element_type=F32)

    @pl.when(t >= nv_ref[0])
    def _():
        o_ref[...] = jnp.zeros_like(o_ref)


def _moe_grouped(kern, a, ws, tile_e, n_valid, *, tn, out_dtype, name):
    M, K = a.shape
    N = ws[0].shape[2]
    T = MOE_TILE
    grid_spec = pltpu.PrefetchScalarGridSpec(
        num_scalar_prefetch=2, grid=(N // tn, M // T),
        in_specs=[pl.BlockSpec((T, K), lambda j, t, te, nv: (t, 0))]
        + [pl.BlockSpec((None, K, tn), lambda j, t, te, nv: (te[t], 0, j))] * len(ws),
        out_specs=pl.BlockSpec((T, tn), lambda j, t, te, nv: (t, j)))
    return pl.pallas_call(
        kern, out_shape=jax.ShapeDtypeStruct((M, N), out_dtype), grid_spec=grid_spec,
        compiler_params=_cparams(("parallel", "arbitrary")), name=name,
    )(tile_e, n_valid, a, *ws)


def _route(table):
    R = table.shape[0]
    E = N_EXPERTS
    T = MOE_TILE
    n_pad = -(-(2 * R + E * (T - 1)) // T) * T
    n_pad = -(-n_pad // GATHER_TILE) * GATHER_TILE
    e_flat = table[:, 8:10].astype(jnp.int32).T.reshape(-1)
    onehot = (e_flat[:, None] == jnp.arange(E, dtype=jnp.int32)[None, :]).astype(jnp.int32)
    csum = jnp.cumsum(onehot, axis=0)
    counts = csum[-1]
    rank = jnp.sum((csum - onehot) * onehot, axis=1)
    padded = ((counts + T - 1) // T) * T
    ends = jnp.cumsum(padded)
    starts = ends - padded
    dest = starts[e_flat] + rank
    order = jnp.argsort(e_flat, stable=True)
    cstart = jnp.cumsum(counts) - counts
    slot = jnp.arange(n_pad, dtype=jnp.int32)
    slot_e = jnp.minimum(jnp.sum((slot[:, None] >= ends[None, :]).astype(jnp.int32), axis=1), E - 1)
    off = slot - starts[slot_e]
    valid = jnp.logical_and(off < counts[slot_e], slot < ends[E - 1])
    src_tok = jnp.where(valid, order[jnp.clip(cstart[slot_e] + off, 0, 2 * R - 1)] % R, 0).astype(jnp.int32)
    tile0 = jnp.arange(n_pad // T, dtype=jnp.int32) * T
    tile_e = jnp.minimum(jnp.sum((tile0[:, None] >= ends[None, :]).astype(jnp.int32), axis=1), E - 1)
    n_valid = (ends[E - 1] // T).astype(jnp.int32).reshape(1)
    return src_tok, dest.astype(jnp.int32), tile_e.astype(jnp.int32), n_valid


def _moe_combine_epi(ya, yb, x, table, g_lat, g_ctx, *, row0, n_lat):
    row = row0 + lax.broadcasted_iota(jnp.int32, (x.shape[0], 1), 0)
    g = jnp.where(row < n_lat, g_lat, g_ctx)
    return x + g * (table[:, 10:11] * ya + table[:, 11:12] * yb)


def _moe(h2, table, w1, w3, w2, xa, mod3, *, n_lat):
    R, D = xa.shape
    src_tok, dest, tile_e, n_valid = _route(table)
    xs = _row_gather(h2, src_tok, n_lists=1, out_dtype=BF16, epi=lambda a, *, row0: a, name="moe_dispatch")
    u = _moe_grouped(_moe_up_kernel, xs, (w1, w3), tile_e, n_valid, tn=512, out_dtype=BF16, name="moe_up")
    y = _moe_grouped(_moe_down_kernel, u, (w2,), tile_e, n_valid, tn=512, out_dtype=F32, name="moe_down")
    g_lat = mod3[0, :, 5 * D:6 * D]
    g_ctx = mod3[1, :, 5 * D:6 * D]
    return _row_gather(y, dest, n_lists=2, out_dtype=F32, epi=functools.partial(_moe_combine_epi, n_lat=n_lat),
                       extras=((xa, "tile"), (table, "tile"), (g_lat, "whole"), (g_ctx, "whole")),
                       name="moe_combine")


def _pack_w_in(w_in, l):
    D = w_in.shape[0]
    offs = np.cumsum([0, 1536, 512, 8, 8, MLA_Q_RANK, MLA_KV_RANK, MLA_ROPE, 1536, 1536, 512])
    seg = lambda i: w_in[:, int(offs[i]):int(offs[i + 1])]
    z = lambda n: jnp.zeros((D, n), w_in.dtype)
    kr = seg(6)
    cols = [seg(0), seg(7), seg(8), seg(1), seg(9),
            seg(4), z(512 - MLA_Q_RANK),
            seg(2), seg(3), z(128 - 16),
            seg(5),
            kr, _rot_cols(kr),
            z(P_COLS - 6528)]
    return jnp.concatenate(cols, axis=1).astype(BF16)


def _rot_cols(w):
    out = []
    for b in range(2):
        blk = w[:, 32 * b:32 * b + 32]
        out += [-blk[:, 16:32], blk[:, 0:16]]
    return jnp.concatenate(out, axis=1)


def _pack_w_uq(w_uq):
    cols = []
    for h in range(N_HEADS):
        base = h * (MLA_NOPE + MLA_ROPE)
        rope = w_uq[:, base + MLA_NOPE:base + MLA_NOPE + MLA_ROPE]
        cols += [w_uq[:, base:base + MLA_NOPE], rope, _rot_cols(rope)]
    w = jnp.concatenate(cols, axis=1)
    return jnp.pad(w, ((0, 512 - MLA_Q_RANK), (0, 0))).astype(BF16)


def _rope_tables(L, Lc):
    def cs(pos, dim):
        inv = ROPE_BASE ** (-jnp.arange(0, dim, 2, dtype=F32) / dim)
        ang = pos.astype(F32)[:, None] * inv[None, :]
        ang = jnp.concatenate([ang, ang], axis=-1)
        return jnp.cos(ang), jnp.sin(ang)
    t = jnp.arange(L, dtype=jnp.int32)
    rows, cols = t // GRID_W, t % GRID_W
    cr, sr = cs(rows, MLA_ROPE // 2)
    cc, sc = cs(cols, MLA_ROPE // 2)
    z64 = jnp.zeros((L, 64), F32)
    mla_cos = jnp.concatenate([cr, cc, z64], axis=1)
    mla_sin = jnp.concatenate([sr, sc, z64], axis=1)
    ctx_cos = jnp.concatenate([jnp.ones((Lc, 64), F32), jnp.zeros((Lc, 64), F32)], axis=1)
    mla_cos = jnp.concatenate([mla_cos, ctx_cos], axis=0)
    mla_sin = jnp.concatenate([mla_sin, jnp.zeros((Lc, 128), F32)], axis=0)
    c, s = cs(t, HEAD_W)
    sign = jnp.concatenate([-jnp.ones((1, 64), F32), jnp.ones((1, 64), F32)], axis=1)
    ret_cos = jnp.concatenate([c, jnp.ones((Lc, HEAD_W), F32)], axis=0)
    ret_sin = jnp.concatenate([s * sign, jnp.zeros((Lc, HEAD_W), F32)], axis=0)
    return mla_cos, mla_sin, ret_cos, ret_sin


def _hy_features(L):
    p = jnp.arange(L, dtype=F32)
    bands = (HY_EMB - 1) // 2
    fr = jnp.linspace(1e-4, bands - 1, bands, dtype=F32)
    ang = (2.0 * math.pi / L) * p[:, None] * fr[None, :]
    z = jnp.concatenate([(p / (L - 1))[:, None], jnp.cos(ang), -jnp.sin(ang)], axis=-1)
    return jnp.pad(z, ((0, 0), (0, 128 - HY_EMB)))


def _hy_filters(L, w1, b1, w2, b2, w3, b3, w4, freq):
    w1p = jnp.pad(w1, ((0, 128 - HY_EMB), (0, 0)))
    row = lambda v: v.reshape(1, -1)
    hd = _hy_mlp(_hy_features(L), w1p, row(b1), w2, row(b2), w3, row(b3), row(freq))
    deltas = jnp.abs(jnp.linspace(math.log(HY_FAST) / HY_TARGET, math.log(HY_SLOW) / HY_TARGET, 2 * GROUP_W, dtype=F32))
    return _hy_filter(hd, w4, deltas.reshape(1, -1))


def _hyena(v, x1, x2, hy_params, hy_bias, *, L, Lc, want_ctx):
    R = v.shape[0]
    C = GROUP_W
    n2 = FFT_N2
    ncol = n2 * C
    tn = _tile(ncol, 8192, 128)
    bias_t = jnp.tile(hy_bias, (1, n2))

    n1 = 2 * L // n2
    kr = n1 // 2
    h_lat = _hy_filters(L, *hy_params)
    f1, g, ginv, f5 = _fft_tables(n1, kr, np.arange(n1 // 4, n1 // 4 + kr))
    h4 = _fft_forward(h_lat.reshape(kr, n2 * 2 * C), f1, g, k_rows=kr, n1=n1, C=2 * C)
    v2, x12, x22 = (t.reshape(R // n2, ncol) for t in (v, x1, x2))

    def lat_conv(u2d, h_cblk, gate_x, gate_v, bias_row, out_dtype):
        u4 = _fft_forward(u2d, f1, g, k_rows=kr, n1=n1, C=C)
        c4 = _fft_inner_mul(ginv, u4, h4, h_cblk)
        return _mm(f5, c4.reshape(2 * n1, ncol), tm=kr, tn=tn, out_dtype=out_dtype, epi=_gate_epi,
                   extras=((gate_x, "full"), (gate_v, "full"), (bias_row, "col")), name="fft_outer_inv")

    z1 = lat_conv(v2, 0, x12, v2, bias_t[0:1], F32)
    y_lat = lat_conv(z1, 1, x22, z1, bias_t[1:2], BF16).reshape(L, C)
    if not want_ctx:
        return jnp.concatenate([y_lat, jnp.zeros((Lc, C), BF16)], axis=0)

    assert Lc == n2
    n1c = 8
    krc = 16
    h_ctx = _hy_filters(Lc, *hy_params)
    f1c, gc, ginvc, f5c = _fft_tables(n1c, krc, np.arange(n1c))
    pad_rows = lambda t: jnp.pad(t.reshape(1, -1), ((0, krc - 1), (0, 0)))
    h4c = _fft_forward(pad_rows(h_ctx), f1c, gc, k_rows=krc, n1=n1c, C=2 * C)
    vc, x1c, x2c = (t[L:] for t in (v, x1, x2))

    def ctx_conv(u, h_cblk):
        u4 = _fft_forward(pad_rows(u), f1c, gc, k_rows=krc, n1=n1c, C=C)
        c4 = _fft_inner_mul(ginvc, u4, h4c, h_cblk)
        y = _mm(f5c, c4.reshape(2 * n1c, ncol), tm=n1c, tn=tn, out_dtype=F32, name="fft_outer_inv_ctx")
        return y.reshape(n1c * n2, C)[Lc // 2:Lc // 2 + Lc]

    z1c = _gate_mul(ctx_conv(vc, 0), x1c, vc, hy_bias[0:1], F32)
    y_ctx = _gate_mul(ctx_conv(z1c, 1), x2c, z1c, hy_bias[1:2], BF16)
    return jnp.concatenate([y_lat, y_ctx], axis=0)


def _gate_mul_kernel(y_ref, x_ref, v_ref, b_ref, o_ref):
    o_ref[...] = (x_ref[...] * (y_ref[...] + b_ref[...] * v_ref[...])).astype(o_ref.dtype)


def _gate_mul(y, x, v, bias_row, out_dtype):
    return pl.pallas_call(_gate_mul_kernel, out_shape=jax.ShapeDtypeStruct(y.shape, out_dtype),
                          name="hy_gate_ctx")(y, x, v, bias_row)


def kernel(x, c, ctx, c_ctx, norm1_w, norm2_w, mod_w, mod_b, w_in, w_out, gdn_conv_w, gdn_A_log, gdn_dt_bias, gdn_norm_w, mla_q_norm_w, mla_w_uq, mla_kv_norm_w, mla_w_ukv, hy_conv_w, hy_w1, hy_b1, hy_w2, hy_b2, hy_w3, hy_b3, hy_w4, hy_freq, hy_bias, ffn_w1, ffn_w3, ffn_w2, moe_router, moe_w1, moe_w3, moe_w2, final_norm_w):
    B, L, D = x.shape
    Lc = ctx.shape[1]
    depth = w_in.shape[0]
    assert B == 1 and D == 4 * GROUP_W and Lc == ROW_TILE and L % ROW_TILE == 0
    R = L + Lc
    n_lat_tiles = L // ROW_TILE
    row = lambda v: v.reshape(1, -1)
    pad_lanes = lambda v: jnp.pad(v.reshape(1, -1), ((0, 0), (0, 128 - v.size)))

    xa = jnp.concatenate([x[0], ctx[0]], axis=0)
    cc = jnp.pad(jnp.stack([c[0], c_ctx], axis=0), ((0, 6), (0, 0)))
    mla_cos, mla_sin, ret_cos, ret_sin = _rope_tables(L, Lc)
    ones_head = jnp.ones((1, HEAD_W), F32)
    tm_big = _tile(R, 1280)

    for l in range(depth):
        want_ctx = l < depth - 1
        mod = _mm(cc, mod_w[l], tm=8, tn=_tile(N_MOD * D, 1024, 128), out_dtype=F32, a_pro=_silu,
                  extras=((row(mod_b[l]), "col"),), epi=lambda acc, b, *, row0: acc + b, name="mod")
        mod3 = mod[0:2].reshape(2, 1, N_MOD * D)

        h = _norm_mod(xa, row(norm1_w[l]), mod3, 1, 0, n_lat_tiles=n_lat_tiles, out_dtype=BF16)
        P = _mm(h, _pack_w_in(w_in[l], l), tm=tm_big, tn=512, out_dtype=F32, name="w_in")

        gq, gk, gv = _conv3(P, gdn_conv_w[l], C_GQKV, n_lat_tiles=n_lat_tiles, gdn=True)
        o_f, o_b = _gdn_scan(gq, gk, gv, P, pad_lanes(gdn_A_log[l]), pad_lanes(gdn_dt_bias[l]), n_lat=L)
        gdn_o = _headnorm_gate(o_f, o_b, P, C_GZ, row(gdn_norm_w[l]))

        qw = jnp.pad(row(mla_q_norm_w[l]), ((0, 0), (0, 512 - MLA_Q_RANK)))
        qh, kh, vh = _mla_proj(P, mla_cos, mla_sin, qw, row(mla_kv_norm_w[l]), _pack_w_uq(mla_w_uq[l]),
                               mla_w_ukv[l].astype(BF16))
        mla_o = _flash(qh, kh, vh, q_row0=0, n_q=L, k_row0=0, n_k=R, tq=_tile(L, 1024), tk=_tile(R, 1280))
        if want_ctx:
            mla_c = _flash(qh, kh, vh, q_row0=L, n_q=Lc, k_row0=L, n_k=Lc, tq=Lc, tk=Lc)
        else:
            mla_c = jnp.zeros((Lc, GROUP_W), BF16)
        mla_o = jnp.concatenate([mla_o, mla_c], axis=0)

        hv, hx1, hx2 = _conv3(P, hy_conv_w[l], C_HY, n_lat_tiles=n_lat_tiles, gdn=False)
        hy_params = (hy_w1[l], hy_b1[l], hy_w2[l], hy_b2[l], hy_w3[l], hy_b3[l], hy_w4[l], hy_freq[l])
        hy_o = _hyena(hv, hx1, hx2, hy_params, hy_bias[l], L=L, Lc=Lc, want_ctx=want_ctx)

        r_f, r_b = _ret_scan(P, ret_cos, ret_sin, n_lat=L)
        ret_o = _headnorm_gate(r_f, r_b, P, C_RG, ones_head)

        xa = _mix_out((gdn_o, mla_o, hy_o, ret_o), w_out[l].astype(BF16), xa, mod3, 2, n_lat=L)

        if l % 2 == 0:
            h2 = _norm_mod(xa, row(norm2_w[l]), mod3, 4, 3, n_lat_tiles=n_lat_tiles, out_dtype=BF16)
            u = _swiglu_up(h2, ffn_w1[l // 2].astype(BF16), ffn_w3[l // 2].astype(BF16))
            xa = _ffn_down(u, ffn_w2[l // 2].astype(BF16), xa, mod3, n_lat=L)
        else:
            router = jnp.pad(moe_router[l // 2], ((0, 0), (0, 128 - N_EXPERTS)))
            h2, table = _norm_mod(xa, row(norm2_w[l]), mod3, 4, 3, n_lat_tiles=n_lat_tiles, out_dtype=F32,
                                  router=router)
            xa = _moe(h2, table, moe_w1[l // 2].astype(BF16), moe_w3[l // 2].astype(BF16),
                      moe_w2[l // 2].astype(BF16), xa, mod3, n_lat=L)

    zero_mod = jnp.zeros((2, 1, N_MOD * D), F32)
    out = _norm_mod(xa, row(final_norm_w), zero_mod, 0, 0, n_lat_tiles=n_lat_tiles, out_dtype=F32, rows=L)
    return out[None]


def _ffn_down(u, w2, xa, mod3, *, n_lat):
    R, D = xa.shape
    g_lat = mod3[0, :, 5 * D:6 * D]
    g_ctx = mod3[1, :, 5 * D:6 * D]
    return _mm(u, w2, tm=_tile(R, 640, 128), tn=512, out_dtype=F32, name="ffn_down",
               extras=((xa, "full"), (g_lat, "col"), (g_ctx, "col")),
               epi=functools.partial(_resid_epi, n_lat=n_lat))
```

```python
import functools
import math

import numpy as np
import jax
import jax.numpy as jnp
from jax import lax
from jax.experimental import pallas as pl
from jax.experimental.pallas import tpu as pltpu

F32 = jnp.float32
BF16 = jnp.bfloat16
HI = lax.Precision.HIGHEST

EPS = 1e-6
GRID_W = 64
N_MOD = 6
HEAD_W = 128
N_HEADS = 4
GROUP_W = N_HEADS * HEAD_W
GDN_CHUNK = 64
GDN_CONV = 3
MLA_NOPE = 128
MLA_ROPE = 64
MLA_Q_RANK = 448
MLA_KV_RANK = 128
ROPE_BASE = 10000.0
HY_EMB = 33
HY_FFN = 64
HY_FAST = 0.3
HY_SLOW = 1.5
HY_TARGET = 1e-2
RET_CHUNK = 128
RET_DECAY_F = (5.0, 7.0, 9.0, 11.0)
RET_DECAY_B = (6.0, 8.0, 10.0, 12.0)
N_EXPERTS = 8
FFT_N2 = 256
ROW_TILE = 256

C_GQKV = 0
C_HY = 1536
C_RQKV = 3072
C_GZ = 4608
C_RG = 5120
C_CQ = 5632
C_GAB = 6144
C_CKV = 6272
C_KR = 6400
P_COLS = 6656

VMEM_LIMIT = 56 * 1024 * 1024
NT_DIMS = (((1,), (1,)), ((), ()))
TN_DIMS = (((0,), (0,)), ((), ()))


def _cparams(sem):
    return pltpu.CompilerParams(dimension_semantics=sem, vmem_limit_bytes=VMEM_LIMIT)


def _tile(n, pref, unit=ROW_TILE):
    best = unit
    t = unit
    while t <= min(n, pref):
        if n % t == 0:
            best = t
        t += unit
    return best


def _silu(x):
    return x * jax.nn.sigmoid(x)


def _bdot(a, b):
    return jnp.dot(a.astype(BF16), b.astype(BF16), preferred_element_type=F32)


def _bdot_nt(a, b):
    return lax.dot_general(a.astype(BF16), b.astype(BF16), NT_DIMS, preferred_element_type=F32)


def _bdot_tn(a, b):
    return lax.dot_general(a.astype(BF16), b.astype(BF16), TN_DIMS, preferred_element_type=F32)


def _mm_kernel(*refs, n_extra, epi, a_pro, precision, tm, order):
    a_ref, b_ref = refs[0], refs[1]
    extras = refs[2:2 + n_extra]
    o_ref = refs[2 + n_extra]
    a = a_ref[...]
    if a_pro is not None:
        a = a_pro(a)
    if precision is None:
        acc = jnp.dot(a.astype(BF16), b_ref[...].astype(BF16), preferred_element_type=F32)
    else:
        acc = jnp.dot(a.astype(F32), b_ref[...].astype(F32), preferred_element_type=F32, precision=precision)
    if epi is not None:
        m = pl.program_id(0 if order == "mn" else 1)
        acc = epi(acc, *[e[...] for e in extras], row0=m * tm)
    o_ref[...] = acc.astype(o_ref.dtype)


def _mm(a, b, *, tm, tn, out_dtype, k=None, m=None, a_col_blk=0, order="mn", extras=(), epi=None,
        a_pro=None, precision=None, name="mm"):
    M = a.shape[0] if m is None else m
    K = a.shape[1] if k is None else k
    N = b.shape[1]
    assert M % tm == 0 and N % tn == 0, (M, tm, N, tn)
    gm, gn = M // tm, N // tn
    if order == "mn":
        grid = (gm, gn)
        mi = lambda i, j: i
        ni = lambda i, j: j
    else:
        grid = (gn, gm)
        mi = lambda i, j: j
        ni = lambda i, j: i
    in_specs = [pl.BlockSpec((tm, K), lambda i, j: (mi(i, j), a_col_blk)),
                pl.BlockSpec((K, tn), lambda i, j: (0, ni(i, j)))]
    ops = [a, b]
    for arr, kind in extras:
        if kind == "full":
            in_specs.append(pl.BlockSpec((tm, tn), lambda i, j: (mi(i, j), ni(i, j))))
        elif kind == "col":
            in_specs.append(pl.BlockSpec((arr.shape[0], tn), lambda i, j: (0, ni(i, j))))
        elif kind == "row":
            in_specs.append(pl.BlockSpec((tm, arr.shape[1]), lambda i, j: (mi(i, j), 0)))
        else:
            raise ValueError(kind)
        ops.append(arr)
    kern = functools.partial(_mm_kernel, n_extra=len(extras), epi=epi, a_pro=a_pro, precision=precision,
                             tm=tm, order=order)
    return pl.pallas_call(
        kern,
        out_shape=jax.ShapeDtypeStruct((M, N), out_dtype),
        grid=grid,
        in_specs=in_specs,
        out_specs=pl.BlockSpec((tm, tn), lambda i, j: (mi(i, j), ni(i, j))),
        compiler_params=_cparams(("parallel", "parallel")),
        name=name,
    )(*ops)


def _norm_mod_kernel(*refs, with_router, d):
    if with_router:
        x_ref, w_ref, sc_ref, sh_ref, r_ref, h_ref, g_ref = refs
    else:
        x_ref, w_ref, sc_ref, sh_ref, h_ref = refs
    x = x_ref[...]
    y = x * lax.rsqrt(jnp.sum(x * x, axis=-1, keepdims=True) * (1.0 / d) + EPS)
    h = (y * w_ref[...]) * (1.0 + sc_ref[...]) + sh_ref[...]
    h_ref[...] = h.astype(h_ref.dtype)
    if with_router:
        logits = jnp.dot(h, r_ref[...], preferred_element_type=F32, precision=HI)
        lane = lax.broadcasted_iota(jnp.int32, logits.shape, 1)
        neg = jnp.float32(-jnp.inf)
        lg = jnp.where(lane < N_EXPERTS, logits, neg)
        m1 = jnp.max(lg, axis=-1, keepdims=True)
        i1 = jnp.min(jnp.where(lg == m1, lane, 128), axis=-1, keepdims=True)
        lg2 = jnp.where(lane == i1, neg, lg)
        m2 = jnp.max(lg2, axis=-1, keepdims=True)
        i2 = jnp.min(jnp.where(lg2 == m2, lane, 128), axis=-1, keepdims=True)
        e2 = jnp.exp(m2 - m1)
        den = 1.0 + e2
        g_ref[...] = (jnp.where(lane == 8, i1.astype(F32), 0.0) + jnp.where(lane == 9, i2.astype(F32), 0.0)
                      + jnp.where(lane == 10, 1.0 / den, 0.0) + jnp.where(lane == 11, e2 / den, 0.0))


def _norm_mod(x, w, mod3, sc_blk, sh_blk, *, n_lat_tiles, out_dtype, router=None, rows=None):
    R = x.shape[0] if rows is None else rows
    D = x.shape[1]
    T = ROW_TILE
    which = lambda i: jnp.where(i >= n_lat_tiles, 1, 0)
    in_specs = [pl.BlockSpec((T, D), lambda i: (i, 0)),
                pl.BlockSpec((1, D), lambda i: (0, 0)),
                pl.BlockSpec((None, 1, D), lambda i: (which(i), 0, sc_blk)),
                pl.BlockSpec((None, 1, D), lambda i: (which(i), 0, sh_blk))]
    ops = [x, w, mod3, mod3]
    out_shape = [jax.ShapeDtypeStruct((R, D), out_dtype)]
    out_specs = [pl.BlockSpec((T, D), lambda i: (i, 0))]
    if router is not None:
        in_specs.append(pl.BlockSpec((D, 128), lambda i: (0, 0)))
        ops.append(router)
        out_shape.append(jax.ShapeDtypeStruct((R, 128), F32))
        out_specs.append(pl.BlockSpec((T, 128), lambda i: (i, 0)))
    res = pl.pallas_call(
        functools.partial(_norm_mod_kernel, with_router=router is not None, d=D),
        out_shape=out_shape, grid=(R // T,), in_specs=in_specs, out_specs=out_specs,
        compiler_params=_cparams(("parallel",)), name="norm_mod",
    )(*ops)
    return res if router is not None else res[0]


def _conv3_kernel(x_ref, prev_ref, next_ref, w_ref, o0, o1, o2, *, n_lat_tiles, gdn):
    i = pl.program_id(0)
    x = x_ref[...]
    T = x.shape[0]
    row = lax.broadcasted_iota(jnp.int32, x.shape, 0)
    has_prev = jnp.logical_and(i != 0, i != n_lat_tiles)
    has_next = jnp.logical_and(i != n_lat_tiles - 1, i != n_lat_tiles)
    prev_row = jnp.where(has_prev, prev_ref[7:8, :], 0.0)
    next_row = jnp.where(has_next, next_ref[0:1, :], 0.0)
    xm = jnp.where(row == 0, prev_row, pltpu.roll(x, 1, 0))
    xp = jnp.where(row == T - 1, next_row, pltpu.roll(x, T - 1, 0))
    w = w_ref[...]
    y = xm * w[0:1, :] + x * w[1:2, :] + xp * w[2:3, :]
    outs = (o0, o1, o2)
    if not gdn:
        for p in range(3):
            outs[p][...] = y[:, p * GROUP_W:(p + 1) * GROUP_W]
        return
    y = _silu(y)
    for p in range(3):
        for h in range(N_HEADS):
            c0 = p * GROUP_W + h * HEAD_W
            t = y[:, c0:c0 + HEAD_W]
            if p < 2:
                t = t * lax.rsqrt(jnp.sum(t * t, axis=-1, keepdims=True) + EPS)
                if p == 0:
                    t = t * (HEAD_W ** -0.5)
            outs[p][:, h * HEAD_W:(h + 1) * HEAD_W] = t


def _conv3(P, w, col0, *, n_lat_tiles, gdn):
    R = P.shape[0]
    T = ROW_TILE
    W = 3 * GROUP_W
    cb = col0 // W
    assert col0 % W == 0
    nb8 = R // 8
    tpb = T // 8
    in_specs = [pl.BlockSpec((T, W), lambda i: (i, cb)),
                pl.BlockSpec((8, W), lambda i: (jnp.maximum(i * tpb - 1, 0), cb)),
                pl.BlockSpec((8, W), lambda i: (jnp.minimum((i + 1) * tpb, nb8 - 1), cb)),
                pl.BlockSpec((GDN_CONV, W), lambda i: (0, 0))]
    return pl.pallas_call(
        functools.partial(_conv3_kernel, n_lat_tiles=n_lat_tiles, gdn=gdn),
        out_shape=[jax.ShapeDtypeStruct((R, GROUP_W), F32)] * 3,
        grid=(R // T,), in_specs=in_specs,
        out_specs=[pl.BlockSpec((T, GROUP_W), lambda i: (i, 0))] * 3,
        compiler_params=_cparams(("parallel",)), name="conv3_gdn" if gdn else "conv3_hy",
    )(P, P, P, w)


def _gdn_kernel(qf, kf, vf, gf, qb, kb_, vb, gb, alog_ref, dtb_ref, of_ref, ob_ref, s_scr):
    C = GDN_CHUNK

    @pl.when(pl.program_id(0) == 0)
    def _():
        s_scr[...] = jnp.zeros_like(s_scr)

    r = lax.broadcasted_iota(jnp.int32, (C, C), 0)
    c = lax.broadcasted_iota(jnp.int32, (C, C), 1)
    eye = (r == c).astype(F32)
    lane = lax.broadcasted_iota(jnp.int32, (C, 128), 1)
    alog = alog_ref[...]
    dtb = dtb_ref[...]
    dirs = ((qf, kf, vf, gf, of_ref, r >= c, r > c), (qb, kb_, vb, gb, ob_ref, r <= c, r < c))
    for d, (q_ref, k_ref, v_ref, g_ref, o_ref, incl, strict) in enumerate(dirs):
        x = g_ref[...]
        t = x + dtb
        softplus = jnp.maximum(t, 0.0) + jnp.log1p(jnp.exp(-jnp.abs(t)))
        g_all = -jnp.exp(alog) * softplus
        beta_all = jax.nn.sigmoid(x)
        cs_all = jnp.dot(incl.astype(F32), g_all, preferred_element_type=F32, precision=HI)
        tot_all = jnp.sum(g_all, axis=0, keepdims=True)
        for h in range(N_HEADS):
            l = N_HEADS * d + h
            gcol = cs_all[:, l:l + 1]
            bcol = beta_all[:, 8 + l:9 + l]
            tot = tot_all[:, l:l + 1]
            q = q_ref[:, h * HEAD_W:(h + 1) * HEAD_W]
            k = k_ref[:, h * HEAD_W:(h + 1) * HEAD_W]
            v = v_ref[:, h * HEAD_W:(h + 1) * HEAD_W]
            u_mat = jnp.where(lane == 0, gcol, jnp.where(lane == 1, 1.0, 0.0))
            v_mat = jnp.where(lane == 0, 1.0, jnp.where(lane == 1, -gcol, 0.0))
            diff = lax.dot_general(u_mat, v_mat, NT_DIMS, preferred_element_type=F32, precision=HI)
            decay = jnp.exp(jnp.where(incl, diff, -1e30))
            kbeta = k * bcol
            a_mat = jnp.where(strict, _bdot_nt(kbeta, k) * decay, 0.0)
            n_mat = -a_mat
            t_mat = eye + n_mat
            p_mat = n_mat
            for _ in range(int(math.log2(C)) - 1):
                p_mat = _bdot(p_mat, p_mat)
                t_mat = t_mat + _bdot(t_mat, p_mat)
            u = _bdot(t_mat, v * bcol)
            w = _bdot(t_mat, kbeta * jnp.exp(gcol))
            s = s_scr[l]
            v_new = u - _bdot(w, s)
            attn = _bdot_nt(q, k) * decay
            o = _bdot(q * jnp.exp(gcol), s) + _bdot(attn, v_new)
            s_scr[l] = s * jnp.exp(tot) + _bdot_tn(k * jnp.exp(tot - gcol), v_new)
            o_ref[:, h * HEAD_W:(h + 1) * HEAD_W] = o


def _gdn_scan(q, k, v, P, alog_row, dtb_row, *, n_lat):
    R = q.shape[0]
    C = GDN_CHUNK
    nt = R // C
    nl = n_lat // C
    gcb = C_GAB // 128
    fwd = lambda i: ((i + nl) % nt, 0)
    bwd = lambda i: (nt - 1 - i, 0)
    gfwd = lambda i: ((i + nl) % nt, gcb)
    gbwd = lambda i: (nt - 1 - i, gcb)
    blk = lambda im: pl.BlockSpec((C, GROUP_W), im)
    in_specs = [blk(fwd), blk(fwd), blk(fwd), pl.BlockSpec((C, 128), gfwd),
                blk(bwd), blk(bwd), blk(bwd), pl.BlockSpec((C, 128), gbwd),
                pl.BlockSpec((1, 128), lambda i: (0, 0)), pl.BlockSpec((1, 128), lambda i: (0, 0))]
    return pl.pallas_call(
        _gdn_kernel,
        out_shape=[jax.ShapeDtypeStruct((R, GROUP_W), F32)] * 2,
        grid=(nt,), in_specs=in_specs, out_specs=[blk(fwd), blk(bwd)],
        scratch_shapes=[pltpu.VMEM((2 * N_HEADS, HEAD_W, HEAD_W), F32)],
        compiler_params=_cparams(("arbitrary",)), name="gdn_scan",
    )(q, k, v, P, q, k, v, P, alog_row, dtb_row)


def _ret_kernel(qf, kf, vf, cf, sf, qb, kb_, vb, cb, sb, dmat_ref, qdec_ref, kdec_ref, cdec_ref,
                of_ref, ob_ref, s_scr):
    @pl.when(pl.program_id(0) == 0)
    def _():
        s_scr[...] = jnp.zeros_like(s_scr)

    dirs = ((qf, kf, vf, cf, sf, of_ref), (qb, kb_, vb, cb, sb, ob_ref))
    for d, (q_ref, k_ref, v_ref, cos_ref, sin_ref, o_ref) in enumerate(dirs):
        cos = cos_ref[...]
        sin = sin_ref[...]
        for h in range(N_HEADS):
            l = N_HEADS * d + h
            sl = slice(h * HEAD_W, (h + 1) * HEAD_W)
            q = q_ref[:, sl]
            k = k_ref[:, sl]
            v = v_ref[:, sl]
            q = q * cos + pltpu.roll(q, HEAD_W // 2, 1) * sin
            k = (k * cos + pltpu.roll(k, HEAD_W // 2, 1) * sin) * (HEAD_W ** -0.5)
            s = s_scr[l]
            inner = _bdot_nt(q, k) * dmat_ref[l]
            o = _bdot(q, s) * qdec_ref[l] + _bdot(inner, v)
            s_scr[l] = s * cdec_ref[l] + _bdot_tn(k * kdec_ref[l], v)
            o_ref[:, sl] = o


def _ret_tables():
    C = RET_CHUNK
    lg = np.concatenate([np.log1p(-np.exp2(-np.asarray(RET_DECAY_F, np.float32)).astype(np.float32)),
                         np.log1p(-np.exp2(-np.asarray(RET_DECAY_B, np.float32)).astype(np.float32))]).astype(np.float64)
    idx = np.arange(C, dtype=np.float64)
    rel = idx[:, None] - idx[None, :]
    dmat = np.zeros((8, C, C))
    qdec = np.zeros((8, C, HEAD_W))
    kdec = np.zeros((8, C, HEAD_W))
    cdec = np.zeros((8, C, HEAD_W))
    for l in range(8):
        low = np.where(rel >= 0, np.exp(np.maximum(rel, 0.0) * lg[l]), 0.0)
        if l < N_HEADS:
            dmat[l] = low
            qdec[l] = np.exp((idx + 1.0) * lg[l])[:, None]
            kdec[l] = np.exp((C - 1.0 - idx) * lg[l])[:, None]
        else:
            dmat[l] = low.T
            qdec[l] = np.exp((C - idx) * lg[l])[:, None]
            kdec[l] = np.exp(idx * lg[l])[:, None]
        cdec[l] = np.exp(C * lg[l])
    return tuple(jnp.asarray(t, F32) for t in (dmat, qdec, kdec, cdec))


def _ret_scan(P, cos_t, sin_t, *, n_lat):
    R = P.shape[0]
    C = RET_CHUNK
    nt = R // C
    nl = n_lat // C
    cb0 = C_RQKV // GROUP_W
    fi = lambda i: (i + nl) % nt
    bi = lambda i: nt - 1 - i
    pblk = lambda f, c: pl.BlockSpec((C, GROUP_W), lambda i: (f(i), c))
    tblk = lambda f: pl.BlockSpec((C, HEAD_W), lambda i: (f(i), 0))
    cst = lambda: pl.BlockSpec((8, C, HEAD_W), lambda i: (0, 0, 0))
    in_specs = [pblk(fi, cb0), pblk(fi, cb0 + 1), pblk(fi, cb0 + 2), tblk(fi), tblk(fi),
                pblk(bi, cb0), pblk(bi, cb0 + 1), pblk(bi, cb0 + 2), tblk(bi), tblk(bi),
                cst(), cst(), cst(), cst()]
    dmat, qdec, kdec, cdec = _ret_tables()
    oblk = lambda f: pl.BlockSpec((C, GROUP_W), lambda i: (f(i), 0))
    return pl.pallas_call(
        _ret_kernel,
        out_shape=[jax.ShapeDtypeStruct((R, GROUP_W), F32)] * 2,
        grid=(nt,), in_specs=in_specs, out_specs=[oblk(fi), oblk(bi)],
        scratch_shapes=[pltpu.VMEM((2 * N_HEADS, HEAD_W, HEAD_W), F32)],
        compiler_params=_cparams(("arbitrary",)), name="ret_scan",
    )(P, P, P, cos_t, sin_t, P, P, P, cos_t, sin_t, dmat, qdec, kdec, cdec)


def _headnorm_kernel(a_ref, b_ref, g_ref, w_ref, o_ref):
    w = w_ref[...]
    for h in range(N_HEADS):
        sl = slice(h * HEAD_W, (h + 1) * HEAD_W)
        o = a_ref[:, sl] + b_ref[:, sl]
        y = o * lax.rsqrt(jnp.sum(o * o, axis=-1, keepdims=True) * (1.0 / HEAD_W) + EPS)
        o_ref[:, sl] = ((y * w) * _silu(g_ref[:, sl])).astype(o_ref.dtype)


def _headnorm_gate(a, b, P, gate_col0, w_row):
    R = a.shape[0]
    T = ROW_TILE
    gb = gate_col0 // GROUP_W
    return pl.pallas_call(
        _headnorm_kernel,
        out_shape=jax.ShapeDtypeStruct((R, GROUP_W), BF16),
        grid=(R // T,),
        in_specs=[pl.BlockSpec((T, GROUP_W), lambda i: (i, 0)), pl.BlockSpec((T, GROUP_W), lambda i: (i, 0)),
                  pl.BlockSpec((T, GROUP_W), lambda i: (i, gb)), pl.BlockSpec((1, HEAD_W), lambda i: (0, 0))],
        out_specs=pl.BlockSpec((T, GROUP_W), lambda i: (i, 0)),
        compiler_params=_cparams(("parallel",)), name="headnorm_gate",
    )(a, b, P, w_row)


def _mla_proj_kernel(cq_ref, ckv_ref, kr_ref, cos_ref, sin_ref, qw_ref, kvw_ref, wuq_ref, wukv_ref,
                     q_out, k_out, v_out):
    cq = cq_ref[...]
    nq = cq * lax.rsqrt(jnp.sum(cq * cq, axis=-1, keepdims=True) * (1.0 / MLA_Q_RANK) + EPS) * qw_ref[...]
    ckv = ckv_ref[...]
    nkv = ckv * lax.rsqrt(jnp.sum(ckv * ckv, axis=-1, keepdims=True) * (1.0 / MLA_KV_RANK) + EPS) * kvw_ref[...]
    q = _bdot(nq, wuq_ref[...])
    kv = _bdot(nkv, wukv_ref[...])
    cos = cos_ref[...]
    sin = sin_ref[...]
    kr = kr_ref[...]
    kr = kr * cos + pltpu.roll(kr, 64, 1) * sin
    scale = (MLA_NOPE + MLA_ROPE) ** -0.5 * math.log2(math.e)
    ones = jnp.ones((cq.shape[0], 128), v_out.dtype)
    for h in range(N_HEADS):
        q_out[:, 256 * h:256 * h + 128] = (q[:, 256 * h:256 * h + 128] * scale).astype(q_out.dtype)
        qr = q[:, 256 * h + 128:256 * h + 256]
        qr = qr * cos + pltpu.roll(qr, 64, 1) * sin
        q_out[:, 256 * h + 128:256 * h + 256] = (qr * scale).astype(q_out.dtype)
        k_out[:, 256 * h:256 * h + 128] = kv[:, 256 * h:256 * h + 128].astype(k_out.dtype)
        k_out[:, 256 * h + 128:256 * h + 256] = kr.astype(k_out.dtype)
        v_out[:, 256 * h:256 * h + 128] = kv[:, 256 * h + 128:256 * h + 256].astype(v_out.dtype)
        v_out[:, 256 * h + 128:256 * h + 256] = ones


def _mla_proj(P, cos_t, sin_t, qw_row, kvw_row, wuq_ext, wukv):
    R = P.shape[0]
    T = ROW_TILE
    cst = lambda s: pl.BlockSpec(s, lambda i: (0, 0))
    return pl.pallas_call(
        _mla_proj_kernel,
        out_shape=[jax.ShapeDtypeStruct((R, 1024), BF16)] * 3,
        grid=(R // T,),
        in_specs=[pl.BlockSpec((T, 512), lambda i: (i, C_CQ // 512)),
                  pl.BlockSpec((T, 128), lambda i: (i, C_CKV // 128)),
                  pl.BlockSpec((T, 128), lambda i: (i, C_KR // 128)),
                  pl.BlockSpec((T, 128), lambda i: (i, 0)), pl.BlockSpec((T, 128), lambda i: (i, 0)),
                  cst((1, 512)), cst((1, 128)), cst((512, 1024)), cst((128, 1024))],
        out_specs=[pl.BlockSpec((T, 1024), lambda i: (i, 0))] * 3,
        compiler_params=_cparams(("parallel",)), name="mla_proj",
    )(P, P, P, cos_t, sin_t, qw_row, kvw_row, wuq_ext, wukv)


def _flash_kernel(q_ref, k_ref, v_ref, o_ref, m_scr, acc_scr, *, nk):
    j = pl.program_id(2)

    @pl.when(j == 0)
    def _():
        m_scr[...] = jnp.full_like(m_scr, -jnp.inf)
        acc_scr[...] = jnp.zeros_like(acc_scr)

    s = lax.dot_general(q_ref[...], k_ref[...], NT_DIMS, preferred_element_type=F32)
    m_prev = m_scr[...]
    m_new = jnp.maximum(m_prev, jnp.max(s, axis=-1, keepdims=True))
    alpha = jnp.exp2(m_prev - m_new)
    p = jnp.exp2(s - m_new).astype(BF16)
    acc_scr[...] = alpha * acc_scr[...] + jnp.dot(p, v_ref[...], preferred_element_type=F32)
    m_scr[...] = m_new

    @pl.when(j == nk - 1)
    def _():
        acc = acc_scr[...]
        o_ref[...] = (acc[:, 0:HEAD_W] / acc[:, HEAD_W:HEAD_W + 1]).astype(o_ref.dtype)


def _flash(qh, kh, vh, *, q_row0, n_q, k_row0, n_k, tq, tk):
    assert n_q % tq == 0 and n_k % tk == 0 and q_row0 % tq == 0 and k_row0 % tk == 0
    nq, nk = n_q // tq, n_k // tk
    qb, kb = q_row0 // tq, k_row0 // tk
    return pl.pallas_call(
        functools.partial(_flash_kernel, nk=nk),
        out_shape=jax.ShapeDtypeStruct((n_q, GROUP_W), BF16),
        grid=(N_HEADS, nq, nk),
        in_specs=[pl.BlockSpec((tq, 256), lambda h, i, j: (qb + i, h)),
                  pl.BlockSpec((tk, 256), lambda h, i, j: (kb + j, h)),
                  pl.BlockSpec((tk, 256), lambda h, i, j: (kb + j, h))],
        out_specs=pl.BlockSpec((tq, HEAD_W), lambda h, i, j: (i, h)),
        scratch_shapes=[pltpu.VMEM((tq, 1), F32), pltpu.VMEM((tq, 2 * HEAD_W), F32)],
        compiler_params=_cparams(("parallel", "parallel", "arbitrary")), name="mla_attention",
    )(qh, kh, vh)


def _hy_mlp_kernel(z_ref, w1, b1, w2, b2, w3, b3, fr, o_ref):
    f = fr[...]
    hdot = lambda a, b: jnp.dot(a, b, preferred_element_type=F32, precision=HI)
    h = jnp.sin(f * (hdot(z_ref[...], w1[...]) + b1[...]))
    h = jnp.sin(f * (hdot(h, w2[...]) + b2[...]))
    o_ref[...] = jnp.sin(f * (hdot(h, w3[...]) + b3[...]))


def _hy_mlp(z, w1p, b1, w2, b2, w3, b3, fr):
    L = z.shape[0]
    T = _tile(L, 2048)
    cst = lambda s: pl.BlockSpec(s, lambda i: (0, 0))
    return pl.pallas_call(
        _hy_mlp_kernel, out_shape=jax.ShapeDtypeStruct((L, HY_FFN), F32), grid=(L // T,),
        in_specs=[pl.BlockSpec((T, 128), lambda i: (i, 0)), cst((128, HY_FFN)), cst((1, HY_FFN)),
                  cst((HY_FFN, HY_FFN)), cst((1, HY_FFN)), cst((HY_FFN, HY_FFN)), cst((1, HY_FFN)),
                  cst((1, HY_FFN))],
        out_specs=pl.BlockSpec((T, HY_FFN), lambda i: (i, 0)),
        compiler_params=_cparams(("parallel",)), name="hy_filter_mlp",
    )(z, w1p, b1, w2, b2, w3, b3, fr)


def _hy_filter_kernel(hd_ref, w4_ref, dl_ref, o_ref, sum_scr, *, L, T):
    ph = pl.program_id(1)
    r = pl.program_id(2)
    h = jnp.dot(hd_ref[...], w4_ref[...], preferred_element_type=F32, precision=HI)
    p = (r * T + lax.broadcasted_iota(jnp.int32, (T, 1), 0)).astype(F32)
    dist = jnp.abs(p - (L // 2)) / L
    h = h * jnp.exp(-dist * dl_ref[...])

    @pl.when(jnp.logical_and(ph == 0, r == 0))
    def _():
        sum_scr[...] = jnp.zeros_like(sum_scr)

    @pl.when(ph == 0)
    def _():
        sum_scr[...] += jnp.sum(jnp.abs(h), axis=0, keepdims=True)

    @pl.when(ph == 1)
    def _():
        o_ref[...] = h / (sum_scr[...] + EPS)


def _hy_filter(hd, w4, deltas_row):
    L = hd.shape[0]
    C2 = w4.shape[1]
    T = _tile(L, 2048)
    tc = 256
    return pl.pallas_call(
        functools.partial(_hy_filter_kernel, L=L, T=T),
        out_shape=jax.ShapeDtypeStruct((L, C2), F32),
        grid=(C2 // tc, 2, L // T),
        in_specs=[pl.BlockSpec((T, HY_FFN), lambda c, p, r: (r, 0)),
                  pl.BlockSpec((HY_FFN, tc), lambda c, p, r: (0, c)),
                  pl.BlockSpec((1, tc), lambda c, p, r: (0, c))],
        out_specs=pl.BlockSpec((T, tc), lambda c, p, r: (r * p, c)),
        scratch_shapes=[pltpu.VMEM((1, tc), F32)],
        compiler_params=_cparams(("arbitrary", "arbitrary", "arbitrary")), name="hy_filter",
    )(hd, w4, deltas_row)


def _cplx_dot(g_ref, pr, pi, o_ref):
    gr = g_ref[0]
    gi = g_ref[1]
    dot = lambda a, b: jnp.dot(a, b, preferred_element_type=F32)
    o_ref[0] = (dot(gr, pr) - dot(gi, pi)).astype(o_ref.dtype)
    o_ref[1] = (dot(gi, pr) + dot(gr, pi)).astype(o_ref.dtype)


def _fft_inner_kernel(g_ref, a_ref, b_ref):
    _cplx_dot(g_ref, a_ref[0], a_ref[1], b_ref)


def _fft_inner_mul_kernel(g_ref, u_ref, h_ref, c_ref):
    ur = u_ref[0].astype(F32)
    ui = u_ref[1].astype(F32)
    hr = h_ref[0].astype(F32)
    hi = h_ref[1].astype(F32)
    pr = (ur * hr - ui * hi).astype(BF16)
    pi = (ur * hi + ui * hr).astype(BF16)
    _cplx_dot(g_ref, pr, pi, c_ref)


def _fft_inner(g, a4, *, tc=GROUP_W):
    _, n1, n2, C = a4.shape
    blk = pl.BlockSpec((2, None, n2, tc), lambda k, c: (0, k, 0, c))
    return pl.pallas_call(
        _fft_inner_kernel,
        out_shape=jax.ShapeDtypeStruct((2, n1, n2, C), BF16),
        grid=(n1, C // tc),
        in_specs=[pl.BlockSpec((None, 2, n2, n2), lambda k, c: (k, 0, 0, 0)), blk],
        out_specs=blk,
        compiler_params=_cparams(("parallel", "parallel")), name="fft_inner",
    )(g, a4)


def _fft_inner_mul(ginv, u4, h4, h_cblk, *, tc=GROUP_W):
    _, n1, n2, C = u4.shape
    assert C == tc
    ublk = pl.BlockSpec((2, None, n2, tc), lambda k: (0, k, 0, 0))
    hblk = pl.BlockSpec((2, None, n2, tc), lambda k: (0, k, 0, h_cblk))
    return pl.pallas_call(
        _fft_inner_mul_kernel,
        out_shape=jax.ShapeDtypeStruct((2, n1, n2, C), BF16),
        grid=(n1,),
        in_specs=[pl.BlockSpec((None, 2, n2, n2), lambda k: (k, 0, 0, 0)), ublk, hblk],
        out_specs=ublk,
        compiler_params=_cparams(("parallel",)), name="fft_inner_mul",
    )(ginv, u4, h4)


def _fft_tables(n1, k_rows, out_rows):
    n2 = FFT_N2
    N = n1 * n2
    k1 = np.arange(n1)
    ang1 = 2.0 * np.pi * ((k1[:, None] * np.arange(k_rows)[None, :]) % n1) / n1
    f1 = np.concatenate([np.cos(ang1), -np.sin(ang1)], axis=0)
    orow = np.asarray(out_rows)
    ang5 = 2.0 * np.pi * ((orow[:, None] * k1[None, :]) % n1) / n1
    f5 = np.concatenate([np.cos(ang5), -np.sin(ang5)], axis=1) / N
    i2 = jnp.arange(n2, dtype=jnp.int32)
    a1 = (2.0 * math.pi / N) * (jnp.arange(n1, dtype=jnp.int32)[:, None] * i2[None, :]).astype(F32)
    a2 = (2.0 * math.pi / n2) * ((i2[:, None] * i2[None, :]) % n2).astype(F32)
    c1, s1, c2, s2 = jnp.cos(a1), jnp.sin(a1), jnp.cos(a2), jnp.sin(a2)
    gr = c1[:, None, :] * c2[None, :, :] - s1[:, None, :] * s2[None, :, :]
    gi = -(s1[:, None, :] * c2[None, :, :] + c1[:, None, :] * s2[None, :, :])
    g = jnp.stack([gr, gi], axis=1).astype(BF16)
    vr = c1[:, :, None] * c2[None, :, :] - s1[:, :, None] * s2[None, :, :]
    vi = s1[:, :, None] * c2[None, :, :] + c1[:, :, None] * s2[None, :, :]
    ginv = jnp.stack([vr, vi], axis=1).astype(BF16)
    return jnp.asarray(f1, BF16), g, ginv, jnp.asarray(f5, BF16)


def _fft_forward(x2d, f1, g, *, k_rows, n1, C, tc=GROUP_W):
    ncol = FFT_N2 * C
    a = _mm(f1, x2d, tm=2 * n1, tn=_tile(ncol, 8192, 128), out_dtype=BF16, k=k_rows, name="fft_outer_fwd")
    return _fft_inner(g, a.reshape(2, n1, FFT_N2, C), tc=tc)


def _gate_epi(acc, x, v, bias, *, row0):
    return x * (acc + bias * v)


def _resid_epi(acc, x, g_lat, g_ctx, *, row0, n_lat):
    row = row0 + lax.broadcasted_iota(jnp.int32, (acc.shape[0], 1), 0)
    return x + jnp.where(row < n_lat, g_lat, g_ctx) * acc


def _mix_out_kernel(a0, a1, a2, a3, w_ref, x_ref, gl_ref, gc_ref, o_ref, *, tm, n_lat):
    acc = jnp.dot(a0[...], w_ref[0:GROUP_W, :], preferred_element_type=F32)
    for g, a in ((1, a1), (2, a2), (3, a3)):
        acc = acc + jnp.dot(a[...], w_ref[g * GROUP_W:(g + 1) * GROUP_W, :], preferred_element_type=F32)
    o_ref[...] = _resid_epi(acc, x_ref[...], gl_ref[...], gc_ref[...], row0=pl.program_id(0) * tm, n_lat=n_lat)


def _mix_out(parts, w_out_b, x, mod3, g_blk, *, n_lat):
    R, D = x.shape
    tm = _tile(R, 1280)
    tn = _tile(D, 1024, 128)
    ablk = pl.BlockSpec((tm, GROUP_W), lambda i, j: (i, 0))
    gspec = lambda which: pl.BlockSpec((None, 1, tn), lambda i, j: (which, 0, g_blk * (D // tn) + j))
    return pl.pallas_call(
        functools.partial(_mix_out_kernel, tm=tm, n_lat=n_lat),
        out_shape=jax.ShapeDtypeStruct((R, D), F32),
        grid=(R // tm, D // tn),
        in_specs=[ablk, ablk, ablk, ablk, pl.BlockSpec((4 * GROUP_W, tn), lambda i, j: (0, j)),
                  pl.BlockSpec((tm, tn), lambda i, j: (i, j)), gspec(0), gspec(1)],
        out_specs=pl.BlockSpec((tm, tn), lambda i, j: (i, j)),
        compiler_params=_cparams(("parallel", "parallel")), name="mix_out",
    )(*parts, w_out_b, x, mod3, mod3)


def _swiglu_up_kernel(a_ref, w1_ref, w3_ref, o_ref):
    a = a_ref[...]
    u = jnp.dot(a, w1_ref[...], preferred_element_type=F32)
    g = jnp.dot(a, w3_ref[...], preferred_element_type=F32)
    o_ref[...] = (_silu(u) * g).astype(o_ref.dtype)


def _swiglu_up(h, w1, w3):
    R, D = h.shape
    F = w1.shape[1]
    tm = _tile(R, 1280)
    tn = _tile(F, 512, 128)
    return pl.pallas_call(
        _swiglu_up_kernel, out_shape=jax.ShapeDtypeStruct((R, F), BF16), grid=(R // tm, F // tn),
        in_specs=[pl.BlockSpec((tm, D), lambda i, j: (i, 0)), pl.BlockSpec((D, tn), lambda i, j: (0, j)),
                  pl.BlockSpec((D, tn), lambda i, j: (0, j))],
        out_specs=pl.BlockSpec((tm, tn), lambda i, j: (i, j)),
        compiler_params=_cparams(("parallel", "parallel")), name="swiglu_up",
    )(h, w1, w3)


MOE_TILE = 512
GATHER_TILE = 256


def _gather_kernel(idx_ref, src_ref, *rest, n_lists, n_extra, T, M, epi):
    extras = rest[:n_extra]
    o_ref, buf, sem = rest[n_extra:n_extra + 3]
    i = pl.program_id(0)

    def row_copy(l, r, tok):
        return pltpu.make_async_copy(src_ref.at[pl.ds(tok, 1), :], buf.at[l, pl.ds(r, 1), :], sem.at[l])

    for l in range(n_lists):
        def issue(r, carry, l=l):
            row_copy(l, r, idx_ref[l * M + i * T + r]).start()
            return carry
        lax.fori_loop(0, T, issue, 0, unroll=8)
    for l in range(n_lists):
        pltpu.make_async_copy(src_ref.at[pl.ds(0, T), :], buf.at[l], sem.at[l]).wait()
    vals = [buf[l] for l in range(n_lists)]
    o_ref[...] = epi(*vals, *[e[...] for e in extras], row0=i * T).astype(o_ref.dtype)


def _row_gather(src, idx, *, n_lists, out_dtype, epi, extras=(), name):
    D = src.shape[1]
    M = idx.shape[0] // n_lists
    T = GATHER_TILE
    assert M % T == 0
    in_specs = [pl.BlockSpec(memory_space=pl.ANY)]
    ops = [src]
    for arr, kind in extras:
        if kind == "tile":
            in_specs.append(pl.BlockSpec((T, arr.shape[1]), lambda i, idx_ref: (i, 0)))
        else:
            in_specs.append(pl.BlockSpec(arr.shape, lambda i, idx_ref: (0, 0)))
        ops.append(arr)
    grid_spec = pltpu.PrefetchScalarGridSpec(
        num_scalar_prefetch=1, grid=(M // T,), in_specs=in_specs,
        out_specs=pl.BlockSpec((T, D), lambda i, idx_ref: (i, 0)),
        scratch_shapes=[pltpu.VMEM((n_lists, T, D), src.dtype), pltpu.SemaphoreType.DMA((n_lists,))])
    return pl.pallas_call(
        functools.partial(_gather_kernel, n_lists=n_lists, n_extra=len(extras), T=T, M=M, epi=epi),
        out_shape=jax.ShapeDtypeStruct((M, D), out_dtype), grid_spec=grid_spec,
        compiler_params=_cparams(("arbitrary",)), name=name,
    )(idx, *ops)


def _moe_up_kernel(te_ref, nv_ref, x_ref, w1_ref, w3_ref, o_ref):
    t = pl.program_id(1)

    @pl.when(t < nv_ref[0])
    def _():
        a = x_ref[...]
        u = jnp.dot(a, w1_ref[...], preferred_element_type=F32)
        g = jnp.dot(a, w3_ref[...], preferred_element_type=F32)
        o_ref[...] = (_silu(u) * g).astype(o_ref.dtype)

    @pl.when(t >= nv_ref[0])
    def _():
        o_ref[...] = jnp.zeros_like(o_ref)


def _moe_down_kernel(te_ref, nv_ref, u_ref, w2_ref, o_ref):
    t = pl.program_id(1)

    @pl.when(t < nv_ref[0])
    def _():
        o_ref[...] = jnp.dot(u_ref[...], w2_ref[...], preferred_element_type=F32)

    @pl.when(t >= nv_ref[0])
    def _():
        o_ref[...] = jnp.zeros_like(o_ref)


def _moe_grouped(kern, a, ws, tile_e, n_valid, *, tn, out_dtype, name):
    M, K = a.shape
    N = ws[0].shape[2]
    T = MOE_TILE
    grid_spec = pltpu.PrefetchScalarGridSpec(
        num_scalar_prefetch=2, grid=(N // tn, M // T),
        in_specs=[pl.BlockSpec((T, K), lambda j, t, te, nv: (t, 0))]
        + [pl.BlockSpec((None, K, tn), lambda j, t, te, nv: (te[t], 0, j))] * len(ws),
        out_specs=pl.BlockSpec((T, tn), lambda j, t, te, nv: (t, j)))
    return pl.pallas_call(
        kern, out_shape=jax.ShapeDtypeStruct((M, N), out_dtype), grid_spec=grid_spec,
        compiler_params=_cparams(("parallel", "arbitrary")), name=name,
    )(tile_e, n_valid, a, *ws)


def _route(table):
    R = table.shape[0]
    E = N_EXPERTS
    T = MOE_TILE
    n_pad = -(-(2 * R + E * (T - 1)) // T) * T
    n_pad = -(-n_pad // GATHER_TILE) * GATHER_TILE
    e_flat = table[:, 8:10].astype(jnp.int32).T.reshape(-1)
    onehot = (e_flat[:, None] == jnp.arange(E, dtype=jnp.int32)[None, :]).astype(jnp.int32)
    csum = jnp.cumsum(onehot, axis=0)
    counts = csum[-1]
    rank = jnp.sum((csum - onehot) * onehot, axis=1)
    padded = ((counts + T - 1) // T) * T
    ends = jnp.cumsum(padded)
    starts = ends - padded
    dest = starts[e_flat] + rank
    order = jnp.argsort(e_flat, stable=True)
    cstart = jnp.cumsum(counts) - counts
    slot = jnp.arange(n_pad, dtype=jnp.int32)
    slot_e = jnp.minimum(jnp.sum((slot[:, None] >= ends[None, :]).astype(jnp.int32), axis=1), E - 1)
    off = slot - starts[slot_e]
    valid = jnp.logical_and(off < counts[slot_e], slot < ends[E - 1])
    src_tok = jnp.where(valid, order[jnp.clip(cstart[slot_e] + off, 0, 2 * R - 1)] % R, 0).astype(jnp.int32)
    tile0 = jnp.arange(n_pad // T, dtype=jnp.int32) * T
    tile_e = jnp.minimum(jnp.sum((tile0[:, None] >= ends[None, :]).astype(jnp.int32), axis=1), E - 1)
    n_valid = (ends[E - 1] // T).astype(jnp.int32).reshape(1)
    return src_tok, dest.astype(jnp.int32), tile_e.astype(jnp.int32), n_valid


def _moe_combine_epi(ya, yb, x, table, g_lat, g_ctx, *, row0, n_lat):
    row = row0 + lax.broadcasted_iota(jnp.int32, (x.shape[0], 1), 0)
    g = jnp.where(row < n_lat, g_lat, g_ctx)
    return x + g * (table[:, 10:11] * ya + table[:, 11:12] * yb)


def _moe(h2, table, w1, w3, w2, xa, mod3, *, n_lat):
    R, D = xa.shape
    src_tok, dest, tile_e, n_valid = _route(table)
    xs = _row_gather(h2, src_tok, n_lists=1, out_dtype=BF16, epi=lambda a, *, row0: a, name="moe_dispatch")
    u = _moe_grouped(_moe_up_kernel, xs, (w1, w3), tile_e, n_valid, tn=512, out_dtype=BF16, name="moe_up")
    y = _moe_grouped(_moe_down_kernel, u, (w2,), tile_e, n_valid, tn=512, out_dtype=F32, name="moe_down")
    g_lat = mod3[0, :, 5 * D:6 * D]
    g_ctx = mod3[1, :, 5 * D:6 * D]
    return _row_gather(y, dest, n_lists=2, out_dtype=F32, epi=functools.partial(_moe_combine_epi, n_lat=n_lat),
                       extras=((xa, "tile"), (table, "tile"), (g_lat, "whole"), (g_ctx, "whole")),
                       name="moe_combine")


def _pack_w_in(w_in, l):
    D = w_in.shape[0]
    offs = np.cumsum([0, 1536, 512, 8, 8, MLA_Q_RANK, MLA_KV_RANK, MLA_ROPE, 1536, 1536, 512])
    seg = lambda i: w_in[:, int(offs[i]):int(offs[i + 1])]
    z = lambda n: jnp.zeros((D, n), w_in.dtype)
    kr = seg(6)
    cols = [seg(0), seg(7), seg(8), seg(1), seg(9),
            seg(4), z(512 - MLA_Q_RANK),
            seg(2), seg(3), z(128 - 16),
            seg(5),
            kr, _rot_cols(kr),
            z(P_COLS - 6528)]
    return jnp.concatenate(cols, axis=1).astype(BF16)


def _rot_cols(w):
    out = []
    for b in range(2):
        blk = w[:, 32 * b:32 * b + 32]
        out += [-blk[:, 16:32], blk[:, 0:16]]
    return jnp.concatenate(out, axis=1)


def _pack_w_uq(w_uq):
    cols = []
    for h in range(N_HEADS):
        base = h * (MLA_NOPE + MLA_ROPE)
        rope = w_uq[:, base + MLA_NOPE:base + MLA_NOPE + MLA_ROPE]
        cols += [w_uq[:, base:base + MLA_NOPE], rope, _rot_cols(rope)]
    w = jnp.concatenate(cols, axis=1)
    return jnp.pad(w, ((0, 512 - MLA_Q_RANK), (0, 0))).astype(BF16)


def _rope_tables(L, Lc):
    def cs(pos, dim):
        inv = ROPE_BASE ** (-jnp.arange(0, dim, 2, dtype=F32) / dim)
        ang = pos.astype(F32)[:, None] * inv[None, :]
        ang = jnp.concatenate([ang, ang], axis=-1)
        return jnp.cos(ang), jnp.sin(ang)
    t = jnp.arange(L, dtype=jnp.int32)
    rows, cols = t // GRID_W, t % GRID_W
    cr, sr = cs(rows, MLA_ROPE // 2)
    cc, sc = cs(cols, MLA_ROPE // 2)
    z64 = jnp.zeros((L, 64), F32)
    mla_cos = jnp.concatenate([cr, cc, z64], axis=1)
    mla_sin = jnp.concatenate([sr, sc, z64], axis=1)
    ctx_cos = jnp.concatenate([jnp.ones((Lc, 64), F32), jnp.zeros((Lc, 64), F32)], axis=1)
    mla_cos = jnp.concatenate([mla_cos, ctx_cos], axis=0)
    mla_sin = jnp.concatenate([mla_sin, jnp.zeros((Lc, 128), F32)], axis=0)
    c, s = cs(t, HEAD_W)
    sign = jnp.concatenate([-jnp.ones((1, 64), F32), jnp.ones((1, 64), F32)], axis=1)
    ret_cos = jnp.concatenate([c, jnp.ones((Lc, HEAD_W), F32)], axis=0)
    ret_sin = jnp.concatenate([s * sign, jnp.zeros((Lc, HEAD_W), F32)], axis=0)
    return mla_cos, mla_sin, ret_cos, ret_sin


def _hy_features(L):
    p = jnp.arange(L, dtype=F32)
    bands = (HY_EMB - 1) // 2
    fr = jnp.linspace(1e-4, bands - 1, bands, dtype=F32)
    ang = (2.0 * math.pi / L) * p[:, None] * fr[None, :]
    z = jnp.concatenate([(p / (L - 1))[:, None], jnp.cos(ang), -jnp.sin(ang)], axis=-1)
    return jnp.pad(z, ((0, 0), (0, 128 - HY_EMB)))


def _hy_filters(L, w1, b1, w2, b2, w3, b3, w4, freq):
    w1p = jnp.pad(w1, ((0, 128 - HY_EMB), (0, 0)))
    row = lambda v: v.reshape(1, -1)
    hd = _hy_mlp(_hy_features(L), w1p, row(b1), w2, row(b2), w3, row(b3), row(freq))
    deltas = jnp.abs(jnp.linspace(math.log(HY_FAST) / HY_TARGET, math.log(HY_SLOW) / HY_TARGET, 2 * GROUP_W, dtype=F32))
    return _hy_filter(hd, w4, deltas.reshape(1, -1))


def _hyena(v, x1, x2, hy_params, hy_bias, *, L, Lc, want_ctx):
    R = v.shape[0]
    C = GROUP_W
    n2 = FFT_N2
    ncol = n2 * C
    tn = _tile(ncol, 8192, 128)
    bias_t = jnp.tile(hy_bias, (1, n2))

    n1 = 2 * L // n2
    kr = n1 // 2
    h_lat = _hy_filters(L, *hy_params)
    f1, g, ginv, f5 = _fft_tables(n1, kr, np.arange(n1 // 4, n1 // 4 + kr))
    h4 = _fft_forward(h_lat.reshape(kr, n2 * 2 * C), f1, g, k_rows=kr, n1=n1, C=2 * C)
    v2, x12, x22 = (t.reshape(R // n2, ncol) for t in (v, x1, x2))

    def lat_conv(u2d, h_cblk, gate_x, gate_v, bias_row, out_dtype):
        u4 = _fft_forward(u2d, f1, g, k_rows=kr, n1=n1, C=C)
        c4 = _fft_inner_mul(ginv, u4, h4, h_cblk)
        return _mm(f5, c4.reshape(2 * n1, ncol), tm=kr, tn=tn, out_dtype=out_dtype, epi=_gate_epi,
                   extras=((gate_x, "full"), (gate_v, "full"), (bias_row, "col")), name="fft_outer_inv")

    z1 = lat_conv(v2, 0, x12, v2, bias_t[0:1], F32)
    y_lat = lat_conv(z1, 1, x22, z1, bias_t[1:2], BF16).reshape(L, C)
    if not want_ctx:
        return jnp.concatenate([y_lat, jnp.zeros((Lc, C), BF16)], axis=0)

    assert Lc == n2
    n1c = 8
    krc = 16
    h_ctx = _hy_filters(Lc, *hy_params)
    f1c, gc, ginvc, f5c = _fft_tables(n1c, krc, np.arange(n1c))
    pad_rows = lambda t: jnp.pad(t.reshape(1, -1), ((0, krc - 1), (0, 0)))
    h4c = _fft_forward(pad_rows(h_ctx), f1c, gc, k_rows=krc, n1=n1c, C=2 * C)
    vc, x1c, x2c = (t[L:] for t in (v, x1, x2))

    def ctx_conv(u, h_cblk):
        u4 = _fft_forward(pad_rows(u), f1c, gc, k_rows=krc, n1=n1c, C=C)
        c4 = _fft_inner_mul(ginvc, u4, h4c, h_cblk)
        y = _mm(f5c, c4.reshape(2 * n1c, ncol), tm=n1c, tn=tn, out_dtype=F32, name="fft_outer_inv_ctx")
        return y.reshape(n1c * n2, C)[Lc // 2:Lc // 2 + Lc]

    z1c = _gate_mul(ctx_conv(vc, 0), x1c, vc, hy_bias[0:1], F32)
    y_ctx = _gate_mul(ctx_conv(z1c, 1), x2c, z1c, hy_bias[1:2], BF16)
    return jnp.concatenate([y_lat, y_ctx], axis=0)


def _gate_mul_kernel(y_ref, x_ref, v_ref, b_ref, o_ref):
    o_ref[...] = (x_ref[...] * (y_ref[...] + b_ref[...] * v_ref[...])).astype(o_ref.dtype)


def _gate_mul(y, x, v, bias_row, out_dtype):
    return pl.pallas_call(_gate_mul_kernel, out_shape=jax.ShapeDtypeStruct(y.shape, out_dtype),
                          name="hy_gate_ctx")(y, x, v, bias_row)


def kernel(x, c, ctx, c_ctx, norm1_w, norm2_w, mod_w, mod_b, w_in, w_out, gdn_conv_w, gdn_A_log, gdn_dt_bias, gdn_norm_w, mla_q_norm_w, mla_w_uq, mla_kv_norm_w, mla_w_ukv, hy_conv_w, hy_w1, hy_b1, hy_w2, hy_b2, hy_w3, hy_b3, hy_w4, hy_freq, hy_bias, ffn_w1, ffn_w3, ffn_w2, moe_router, moe_w1, moe_w3, moe_w2, final_norm_w):
    B, L, D = x.shape
    Lc = ctx.shape[1]
    depth = w_in.shape[0]
    assert B == 1 and D == 4 * GROUP_W and Lc == ROW_TILE and L % ROW_TILE == 0
    R = L + Lc
    n_lat_tiles = L // ROW_TILE
    row = lambda v: v.reshape(1, -1)
    pad_lanes = lambda v: jnp.pad(v.reshape(1, -1), ((0, 0), (0, 128 - v.size)))

    xa = jnp.concatenate([x[0], ctx[0]], axis=0)
    cc = jnp.pad(jnp.stack([c[0], c_ctx], axis=0), ((0, 6), (0, 0)))
    mla_cos, mla_sin, ret_cos, ret_sin = _rope_tables(L, Lc)
    ones_head = jnp.ones((1, HEAD_W), F32)
    tm_big = _tile(R, 1280)

    for l in range(depth):
        want_ctx = l < depth - 1
        mod = _mm(cc, mod_w[l], tm=8, tn=_tile(N_MOD * D, 1024, 128), out_dtype=F32, a_pro=_silu,
                  extras=((row(mod_b[l]), "col"),), epi=lambda acc, b, *, row0: acc + b, name="mod")
        mod3 = mod[0:2].reshape(2, 1, N_MOD * D)

        h = _norm_mod(xa, row(norm1_w[l]), mod3, 1, 0, n_lat_tiles=n_lat_tiles, out_dtype=BF16)
        P = _mm(h, _pack_w_in(w_in[l], l), tm=tm_big, tn=512, out_dtype=F32, name="w_in")

        gq, gk, gv = _conv3(P, gdn_conv_w[l], C_GQKV, n_lat_tiles=n_lat_tiles, gdn=True)
        o_f, o_b = _gdn_scan(gq, gk, gv, P, pad_lanes(gdn_A_log[l]), pad_lanes(gdn_dt_bias[l]), n_lat=L)
        gdn_o = _headnorm_gate(o_f, o_b, P, C_GZ, row(gdn_norm_w[l]))

        qw = jnp.pad(row(mla_q_norm_w[l]), ((0, 0), (0, 512 - MLA_Q_RANK)))
        qh, kh, vh = _mla_proj(P, mla_cos, mla_sin, qw, row(mla_kv_norm_w[l]), _pack_w_uq(mla_w_uq[l]),
                               mla_w_ukv[l].astype(BF16))
        mla_o = _flash(qh, kh, vh, q_row0=0, n_q=L, k_row0=0, n_k=R, tq=_tile(L, 1024), tk=_tile(R, 1280))
        if want_ctx:
            mla_c = _flash(qh, kh, vh, q_row0=L, n_q=Lc, k_row0=L, n_k=Lc, tq=Lc, tk=Lc)
        else:
            mla_c = jnp.zeros((Lc, GROUP_W), BF16)
        mla_o = jnp.concatenate([mla_o, mla_c], axis=0)

        hv, hx1, hx2 = _conv3(P, hy_conv_w[l], C_HY, n_lat_tiles=n_lat_tiles, gdn=False)
        hy_params = (hy_w1[l], hy_b1[l], hy_w2[l], hy_b2[l], hy_w3[l], hy_b3[l], hy_w4[l], hy_freq[l])
        hy_o = _hyena(hv, hx1, hx2, hy_params, hy_bias[l], L=L, Lc=Lc, want_ctx=want_ctx)

        r_f, r_b = _ret_scan(P, ret_cos, ret_sin, n_lat=L)
        ret_o = _headnorm_gate(r_f, r_b, P, C_RG, ones_head)

        xa = _mix_out((gdn_o, mla_o, hy_o, ret_o), w_out[l].astype(BF16), xa, mod3, 2, n_lat=L)

        if l % 2 == 0:
            h2 = _norm_mod(xa, row(norm2_w[l]), mod3, 4, 3, n_lat_tiles=n_lat_tiles, out_dtype=BF16)
            u = _swiglu_up(h2, ffn_w1[l // 2].astype(BF16), ffn_w3[l // 2].astype(BF16))
            xa = _ffn_down(u, ffn_w2[l // 2].astype(BF16), xa, mod3, n_lat=L)
        else:
            router = jnp.pad(moe_router[l // 2], ((0, 0), (0, 128 - N_EXPERTS)))
            h2, table = _norm_mod(xa, row(norm2_w[l]), mod3, 4, 3, n_lat_tiles=n_lat_tiles, out_dtype=F32,
                                  router=router)
            xa = _moe(h2, table, moe_w1[l // 2].astype(BF16), moe_w3[l // 2].astype(BF16),
                      moe_w2[l // 2].astype(BF16), xa, mod3, n_lat=L)

    zero_mod = jnp.zeros((2, 1, N_MOD * D), F32)
    out = _norm_mod(xa, row(final_norm_w), zero_mod, 0, 0, n_lat_tiles=n_lat_tiles, out_dtype=F32, rows=L)
    return out[None]


def _ffn_down(u, w2, xa, mod3, *, n_lat):
    R, D = xa.shape
    g_lat = mod3[0, :, 5 * D:6 * D]
    g_ctx = mod3[1, :, 5 * D:6 * D]
    return _mm(u, w2, tm=_tile(R, 640, 128), tn=512, out_dtype=F32, name="ffn_down",
               extras=((xa, "full"), (g_lat, "col"), (g_ctx, "col")),
               epi=functools.partial(_resid_epi, n_lat=n_lat))
```
